```python
import math
import jax, jax.numpy as jnp
from jax import lax
import numpy as np

D_MODEL = 1024
BATCH = 2
SEQ = 8192
DEPTH = 4
DEC_BATCH = 128
DEC_SEQ = 8
PAST_LEN = 8192
PAGE_SIZE = 128

N_A_LAYERS = DEPTH // 2
N_B_LAYERS = DEPTH - N_A_LAYERS
A_HEADS = 8
A_DK = D_MODEL // A_HEADS
A_DV = D_MODEL // A_HEADS
A_CHUNK = 64
B_HEAD_DIM = 64
B_HEADS = D_MODEL // B_HEAD_DIM
B_KV_HEADS = max(1, B_HEADS // 8)
B_GROUP = B_HEADS // B_KV_HEADS
WINDOW = 128
D_FF = 256 * ((8 * D_MODEL // 3 + 255) // 256)
EPS = 1e-6
NEG_INF = -1e30

kernel_name = "yoco_hgrn2_swa_sink_macaron"

F32 = jnp.float32


def rms_norm(x, g):
    xf = x.astype(F32)
    y = xf * lax.rsqrt(jnp.mean(xf * xf, axis=-1, keepdims=True) + EPS)
    return (y * g.astype(F32)).astype(x.dtype)


def swiglu(h, w_in, w_out):
    gate, up = jnp.split(h @ w_in, 2, axis=-1)
    return (jax.nn.silu(gate) * up) @ w_out


def _heads(x, n_heads):
    return x.reshape(x.shape[:-1] + (n_heads, x.shape[-1] // n_heads))


def hgrn2_scan(q, k, v, log_f, s0):
    b, t, h, dk = q.shape
    dv = v.shape[-1]
    c = math.gcd(t, A_CHUNK)
    n = t // c

    def to_chunks(a):
        return a.reshape(b, n, c, h, a.shape[-1]).transpose(1, 0, 3, 2, 4)

    causal = jnp.tril(jnp.ones((c, c), dtype=bool))

    def step(s, inp):
        qc, kc, vc, gc = inp
        cum = jnp.cumsum(gc, axis=2)
        diff = cum[:, :, :, None, :] - cum[:, :, None, :, :]
        decay = jnp.where(causal[:, :, None], jnp.exp(jnp.minimum(diff, 0.0)), 0.0)
        scores = jnp.einsum('bhtd,bhsd,bhtsd->bhts', qc, kc, decay)
        o = (jnp.einsum('bhts,bhsv->bhtv', scores, vc)
             + jnp.einsum('bhtd,bhdv->bhtv', qc * jnp.exp(cum), s))
        last = cum[:, :, -1:, :]
        s_new = (jnp.exp(last[:, :, 0, :, None]) * s
                 + jnp.einsum('bhsd,bhsv->bhdv', kc * jnp.exp(last - cum), vc))
        return s_new, o

    s_final, o = lax.scan(step, s0, (to_chunks(q), to_chunks(k), to_chunks(v), to_chunks(log_f)))
    o = o.transpose(1, 0, 3, 2, 4).reshape(b, t, h, dv)
    return o, s_final


def hgrn2_mixer(h, w_in, lb, g_norm, w_out, s0):
    bsz, t, _ = h.shape
    q, z, i, g = jnp.split(h @ w_in, 4, axis=-1)
    lb = lb.reshape(A_HEADS, A_DK)
    z = _heads(z.astype(F32), A_HEADS)
    pos = lb > 0
    log_lb = jnp.where(pos, jnp.log(jnp.where(pos, lb, 1.0)), NEG_INF)
    log_f = jnp.logaddexp(log_lb, jnp.log1p(-lb) + jax.nn.log_sigmoid(z))
    k = (1.0 - lb) * jax.nn.sigmoid(-z)
    o, s = hgrn2_scan(_heads(q.astype(F32), A_HEADS), k, _heads(i.astype(F32), A_HEADS),
                      log_f, s0.astype(F32))
    o = rms_norm(o, g_norm) * jax.nn.silu(_heads(g.astype(F32), A_HEADS))
    return o.reshape(bsz, t, D_MODEL).astype(h.dtype) @ w_out, s


def shared_kv(x, kv_norm, kv_w, kv_b):
    hk = rms_norm(x, kv_norm)
    k, v = jnp.split(hk @ kv_w + kv_b, 2, axis=-1)
    return _heads(k, B_KV_HEADS), _heads(v, B_KV_HEADS)


def sink_attend(q, k, v, qpos, kpos, sinks):
    b, n, tq = q.shape[:3]
    qg = q.reshape(b, n, tq, B_KV_HEADS, B_GROUP, B_HEAD_DIM).astype(F32)
    s = jnp.einsum('bnqkgd,bnskd->bnkgqs', qg, k.astype(F32)) * (B_HEAD_DIM ** -0.5)
    rel = qpos[:, :, None] - kpos[:, None, :]
    mask = (rel >= 0) & (rel <= WINDOW) & (kpos[:, None, :] >= 0)
    s = jnp.where(mask[None, :, None, None], s, NEG_INF)
    sink = sinks.astype(F32).reshape(B_KV_HEADS, B_GROUP)[None, None, :, :, None, None]
    m = jnp.maximum(jnp.max(s, axis=-1, keepdims=True), sink)
    p = jnp.exp(s - m)
    p = p / (jnp.sum(p, axis=-1, keepdims=True) + jnp.exp(sink - m))
    o = jnp.einsum('bnkgqs,bnskd->bnqkgd', p, v.astype(F32))
    return o.reshape(b, n * tq, B_HEADS * B_HEAD_DIM).astype(q.dtype)


def swa_mixer(h, k, v, w_q, b_q, sinks, w_o, b_o, k_past, v_past):
    bsz, t, _ = h.shape
    q = _heads(h @ w_q + b_q, B_HEADS)
    if k_past is None:
        n = t // WINDOW
        pad = ((0, 0), (WINDOW, 0), (0, 0), (0, 0))
        kb = jnp.pad(k, pad).reshape(bsz, n + 1, WINDOW, B_KV_HEADS, B_HEAD_DIM)
        vb = jnp.pad(v, pad).reshape(bsz, n + 1, WINDOW, B_KV_HEADS, B_HEAD_DIM)
        kcat = jnp.concatenate([kb[:, :-1], kb[:, 1:]], axis=2)
        vcat = jnp.concatenate([vb[:, :-1], vb[:, 1:]], axis=2)
        kp = jnp.arange(-WINDOW, t).reshape(n + 1, WINDOW)
        kpos = jnp.concatenate([kp[:-1], kp[1:]], axis=1)
        qpos = jnp.arange(t).reshape(n, WINDOW)
        o = sink_attend(q.reshape(bsz, n, WINDOW, B_HEADS, B_HEAD_DIM), kcat, vcat, qpos, kpos, sinks)
    else:
        k_all = jnp.concatenate([k_past.astype(k.dtype), k], axis=1)[:, None]
        v_all = jnp.concatenate([v_past.astype(v.dtype), v], axis=1)[:, None]
        qp = PAST_LEN + jnp.arange(t)
        kpos = jnp.concatenate([PAST_LEN - WINDOW + jnp.arange(WINDOW), qp])[None]
        o = sink_attend(q[:, None], k_all, v_all, qp[None], kpos, sinks)
    return o @ w_o + b_o


def setup_inputs(seed: int = 0) -> dict:
    key = jax.random.key(seed)
    ks = jax.random.split(key, 20)
    nrm = lambda k, shape, s: jax.random.normal(k, shape, F32) * s
    qkv_w = B_HEADS * B_HEAD_DIM
    return {
        "x_prompt": nrm(ks[0], (BATCH, SEQ, D_MODEL), 1.0),
        "x_sample": nrm(ks[1], (DEC_BATCH, DEC_SEQ, D_MODEL), 1.0),
        "state_hgrn": nrm(ks[2], (N_A_LAYERS, DEC_BATCH, A_HEADS, A_DK, A_DV), 0.5),
        "cache_k_window": nrm(ks[3], (DEC_BATCH, WINDOW, B_KV_HEADS, B_HEAD_DIM), 1.0),
        "cache_v_window": nrm(ks[4], (DEC_BATCH, WINDOW, B_KV_HEADS, B_HEAD_DIM), 1.0),
        "norm_g": 1.0 + nrm(ks[5], (DEPTH, 6, D_MODEL), 0.02),
        "ffn_w_in": nrm(ks[6], (DEPTH, 2, D_MODEL, 2 * D_FF), D_MODEL ** -0.5),
        "ffn_w_out": nrm(ks[7], (DEPTH, 2, D_FF, D_MODEL), D_FF ** -0.5),
        "a_w_in": nrm(ks[8], (N_A_LAYERS, D_MODEL, 4 * D_MODEL), D_MODEL ** -0.5),
        "a_lower_bounds": nrm(ks[9], (N_A_LAYERS, A_HEADS * A_DK), 1.0),
        "a_gnorm": 1.0 + nrm(ks[10], (N_A_LAYERS, A_DV), 0.02),
        "a_w_out": nrm(ks[11], (N_A_LAYERS, D_MODEL, D_MODEL), D_MODEL ** -0.5),
        "kv_norm": 1.0 + nrm(ks[12], (D_MODEL,), 0.02),
        "kv_w": nrm(ks[13], (D_MODEL, 2 * B_KV_HEADS * B_HEAD_DIM), D_MODEL ** -0.5),
        "kv_b": nrm(ks[14], (2 * B_KV_HEADS * B_HEAD_DIM,), 0.02),
        "b_w_q": nrm(ks[15], (N_B_LAYERS, D_MODEL, qkv_w), D_MODEL ** -0.5),
        "b_b_q": nrm(ks[16], (N_B_LAYERS, qkv_w), 0.02),
        "b_sinks": nrm(ks[17], (N_B_LAYERS, B_HEADS), 1.0),
        "b_w_o": nrm(ks[18], (N_B_LAYERS, qkv_w, D_MODEL), qkv_w ** -0.5),
        "b_b_o": nrm(ks[19], (N_B_LAYERS, D_MODEL), 0.02),
    }


def reference(x_prompt, x_sample, state_hgrn, cache_k_window, cache_v_window,
              norm_g, ffn_w_in, ffn_w_out, a_w_in, a_lower_bounds, a_gnorm, a_w_out,
              kv_norm, kv_w, kv_b, b_w_q, b_b_q, b_sinks, b_w_o, b_b_o):
    lb_soft = jax.nn.softmax(a_lower_bounds.astype(F32), axis=0)
    lbs = jnp.cumsum(lb_soft, axis=0) - lb_soft[0]

    def trunk(x, s_init, k_past, v_past):
        new_s = []
        k = v = None
        for l in range(DEPTH):
            g = norm_g[l]
            x = x + 0.5 * rms_norm(swiglu(rms_norm(x, g[0]), ffn_w_in[l, 0], ffn_w_out[l, 0]), g[1])
            h = rms_norm(x, g[2])
            if l < N_A_LAYERS:
                m, s = hgrn2_mixer(h, a_w_in[l], lbs[l], a_gnorm[l], a_w_out[l], s_init[l])
                new_s.append(s)
            else:
                j = l - N_A_LAYERS
                m = swa_mixer(h, k, v, b_w_q[j], b_b_q[j], b_sinks[j], b_w_o[j], b_b_o[j],
                              k_past, v_past)
            x = x + rms_norm(m, g[3])
            x = x + 0.5 * rms_norm(swiglu(rms_norm(x, g[4]), ffn_w_in[l, 1], ffn_w_out[l, 1]), g[5])
            if l == N_A_LAYERS - 1:
                k, v = shared_kv(x, kv_norm, kv_w, kv_b)
        return x, jnp.stack(new_s), k, v

    s0_prompt = jnp.zeros((N_A_LAYERS, x_prompt.shape[0], A_HEADS, A_DK, A_DV), F32)
    y_prompt, s_prompt, k_p, v_p = trunk(x_prompt, s0_prompt, None, None)
    y_sample, s_sample, k_s, v_s = trunk(x_sample, state_hgrn, cache_k_window, cache_v_window)

    k_win_prompt = k_p[:, -WINDOW:]
    v_win_prompt = v_p[:, -WINDOW:]
    k_win_sample = jnp.concatenate([cache_k_window.astype(k_s.dtype), k_s], axis=1)[:, -WINDOW:]
    v_win_sample = jnp.concatenate([cache_v_window.astype(v_s.dtype), v_s], axis=1)[:, -WINDOW:]
    return (y_prompt, y_sample, s_prompt, s_sample, k_win_prompt, v_win_prompt, k_win_sample, v_win_sample)
```

```python
import functools
import math

import jax
import jax.numpy as jnp
from jax import lax
from jax.experimental import pallas as pl
from jax.experimental.pallas import tpu as pltpu

F32 = jnp.float32
BF16 = jnp.bfloat16

D_MODEL = 1024
DEPTH = 4
N_A_LAYERS = 2
A_HEADS = 8
A_DK = 128
A_DV = 128
B_HEAD_DIM = 64
B_HEADS = 16
B_KV_HEADS = 2
B_GROUP = 8
WINDOW = 128
D_FF = 2816
EPS = 1e-6
NEG_INF = -1e30

VMEM_LIMIT_BYTES = 56 * 1024 * 1024

ROW_TILE = 1024
FFN_TILE = 256
PROMPT_CHUNK = 64
SAMPLE_SEQS_PER_STEP = 8


def _rms(x, g):
    return x * lax.rsqrt(jnp.mean(x * x, axis=-1, keepdims=True) + EPS) * g


def _sigmoid(x):
    return 1.0 / (1.0 + jnp.exp(-x))


def _cparams(*sem):
    return pltpu.CompilerParams(dimension_semantics=sem, vmem_limit_bytes=VMEM_LIMIT_BYTES)


def _ffn_body(x_ref, gpre_ref, gpost_ref, wg_ref, wu_ref, wo_ref, out_ref, h_ref, acc_ref, *, nf):
    f = pl.program_id(1)

    @pl.when(f == 0)
    def _():
        h_ref[...] = _rms(x_ref[...], gpre_ref[...]).astype(BF16)

    h = h_ref[...]
    gate = jnp.dot(h, wg_ref[...], preferred_element_type=F32)
    up = jnp.dot(h, wu_ref[...], preferred_element_type=F32)
    act = (gate * _sigmoid(gate) * up).astype(BF16)
    part = jnp.dot(act, wo_ref[...], preferred_element_type=F32)

    @pl.when(f == 0)
    def _():
        acc_ref[...] = part

    @pl.when(f > 0)
    def _():
        acc_ref[...] += part

    @pl.when(f == nf - 1)
    def _():
        out_ref[...] = x_ref[...] + 0.5 * _rms(acc_ref[...], gpost_ref[...])


def _ffn(x, g_pre, g_post, w_in, w_out):
    n = x.shape[0]
    nf = D_FF // FFN_TILE
    return pl.pallas_call(
        functools.partial(_ffn_body, nf=nf),
        grid=(n // ROW_TILE, nf),
        in_specs=[
            pl.BlockSpec((ROW_TILE, D_MODEL), lambda i, f: (i, 0)),
            pl.BlockSpec((1, D_MODEL), lambda i, f: (0, 0)),
            pl.BlockSpec((1, D_MODEL), lambda i, f: (0, 0)),
            pl.BlockSpec((D_MODEL, FFN_TILE), lambda i, f: (0, f)),
            pl.BlockSpec((D_MODEL, FFN_TILE), lambda i, f: (0, nf + f)),
            pl.BlockSpec((FFN_TILE, D_MODEL), lambda i, f: (f, 0)),
        ],
        out_specs=pl.BlockSpec((ROW_TILE, D_MODEL), lambda i, f: (i, 0)),
        out_shape=jax.ShapeDtypeStruct((n, D_MODEL), F32),
        scratch_shapes=[pltpu.VMEM((ROW_TILE, D_MODEL), BF16), pltpu.VMEM((ROW_TILE, D_MODEL), F32)],
        compiler_params=_cparams("parallel", "arbitrary"),
        name="ffn",
    )(x, g_pre, g_post, w_in, w_in, w_out)


def _pre_mm_body(x_ref, g_ref, w_ref, b_ref, out_ref, h_ref):
    @pl.when(pl.program_id(1) == 0)
    def _():
        h_ref[...] = _rms(x_ref[...], g_ref[...]).astype(BF16)

    out_ref[...] = jnp.dot(h_ref[...], w_ref[...], preferred_element_type=F32) + b_ref[...]


def _pre_mm(x, g, w, b, name):
    n = x.shape[0]
    n_out = w.shape[1]
    tn = min(n_out, 1024)
    return pl.pallas_call(
        _pre_mm_body,
        grid=(n // ROW_TILE, n_out // tn),
        in_specs=[
            pl.BlockSpec((ROW_TILE, D_MODEL), lambda i, j: (i, 0)),
            pl.BlockSpec((1, D_MODEL), lambda i, j: (0, 0)),
            pl.BlockSpec((D_MODEL, tn), lambda i, j: (0, j)),
            pl.BlockSpec((1, tn), lambda i, j: (0, j)),
        ],
        out_specs=pl.BlockSpec((ROW_TILE, tn), lambda i, j: (i, j)),
        out_shape=jax.ShapeDtypeStruct((n, n_out), F32),
        scratch_shapes=[pltpu.VMEM((ROW_TILE, D_MODEL), BF16)],
        compiler_params=_cparams("parallel", "arbitrary"),
        name=name,
    )(x, g, w, b)


def _post_mm_body(o_ref, w_ref, b_ref, g_ref, x_ref, out_ref):
    m = jnp.dot(o_ref[...], w_ref[...], preferred_element_type=F32) + b_ref[...]
    out_ref[...] = x_ref[...] + _rms(m, g_ref[...])


def _post_mm(o, w, b, g, x, name):
    n = x.shape[0]
    return pl.pallas_call(
        _post_mm_body,
        grid=(n // ROW_TILE,),
        in_specs=[
            pl.BlockSpec((ROW_TILE, D_MODEL), lambda i: (i, 0)),
            pl.BlockSpec((D_MODEL, D_MODEL), lambda i: (0, 0)),
            pl.BlockSpec((1, D_MODEL), lambda i: (0, 0)),
            pl.BlockSpec((1, D_MODEL), lambda i: (0, 0)),
            pl.BlockSpec((ROW_TILE, D_MODEL), lambda i: (i, 0)),
        ],
        out_specs=pl.BlockSpec((ROW_TILE, D_MODEL), lambda i: (i, 0)),
        out_shape=jax.ShapeDtypeStruct((n, D_MODEL), F32),
        compiler_params=_cparams("parallel"),
        name=name,
    )(o, w, b, g, x)


def _split3(a):
    a1 = a.astype(BF16)
    r1 = a - a1.astype(F32)
    a2 = r1.astype(BF16)
    a3 = (r1 - a2.astype(F32)).astype(BF16)
    return a1, a2, a3


def _hgrn_body(*refs, chunk, has_init):
    if has_init:
        (q_ref, z_ref, i_ref, g_ref, loglb_ref, log1mlb_ref, onemlb_ref, gn_ref, s0_ref,
         o_ref, sout_ref, st_ref) = refs
    else:
        (q_ref, z_ref, i_ref, g_ref, loglb_ref, log1mlb_ref, onemlb_ref, gn_ref,
         o_ref, sout_ref, st_ref) = refs
        s0_ref = None
    c = pl.program_id(1)
    n_chunks = pl.num_programs(1)

    @pl.when(c == 0)
    def _():
        if has_init:
            for h in range(A_HEADS):
                st_ref[h] = s0_ref[0, h].T
        else:
            st_ref[...] = jnp.zeros_like(st_ref)

    row = lax.broadcasted_iota(jnp.int32, (chunk, chunk), 0)
    col = lax.broadcasted_iota(jnp.int32, (chunk, chunk), 1)
    causal = col <= row
    tril = causal.astype(BF16)
    mid = chunk // 2
    gn = gn_ref[...]

    for h in range(A_HEADS):
        sl = slice(h * A_DK, (h + 1) * A_DK)
        q = q_ref[:, sl]
        z = z_ref[:, sl]
        v = i_ref[:, sl]
        g = g_ref[:, sl]
        log_lb = loglb_ref[:, sl]
        log1m_lb = log1mlb_ref[:, sl]
        onem_lb = onemlb_ref[:, sl]

        e = jnp.exp(-jnp.abs(z))
        r = 1.0 / (1.0 + e)
        log_sig = jnp.minimum(z, 0.0) - jnp.log1p(e)
        b_ = log1m_lb + log_sig
        log_f = jnp.maximum(log_lb, b_) + jnp.log1p(jnp.exp(-jnp.abs(log_lb - b_)))
        k = onem_lb * jnp.where(z >= 0.0, e * r, r)

        f1, f2, f3 = _split3(log_f)
        cum = (jnp.dot(tril, f1, preferred_element_type=F32)
               + jnp.dot(tril, f2, preferred_element_type=F32)
               + jnp.dot(tril, f3, preferred_element_type=F32))
        last = cum[chunk - 1:chunk, :]
        ref_row = cum[mid:mid + 1, :]

        qs = (q * jnp.exp(cum - ref_row)).astype(BF16)
        ks = (k * jnp.exp(ref_row - cum)).astype(BF16)
        scores = lax.dot_general(qs, ks, (((1,), (1,)), ((), ())), preferred_element_type=F32)
        scores = jnp.where(causal, scores, 0.0).astype(BF16)
        vb = v.astype(BF16)
        st = st_ref[h]
        qe = (q * jnp.exp(cum)).astype(BF16)
        o = (jnp.dot(scores, vb, preferred_element_type=F32)
             + lax.dot_general(qe, st.astype(BF16), (((1,), (1,)), ((), ())),
                               preferred_element_type=F32))
        ke = (k * jnp.exp(last - cum)).astype(BF16)
        st_ref[h] = (jnp.exp(last) * st
                     + lax.dot_general(vb, ke, (((0,), (0,)), ((), ())), preferred_element_type=F32))

        o_ref[:, sl] = (_rms(o, gn) * (g * _sigmoid(g))).astype(o_ref.dtype)

    @pl.when(c == n_chunks - 1)
    def _():
        for h in range(A_HEADS):
            sout_ref[0, h] = st_ref[h].T


def _hgrn_scan(p, lb_terms, gn, s0, *, row0, n_seq, seq_len, chunk, name):
    n_chunks = seq_len // chunk
    blk0 = row0 // chunk
    has_init = s0 is not None

    def rows(b, c):
        return blk0 + b * n_chunks + c

    in_specs = [pl.BlockSpec((chunk, D_MODEL), functools.partial(lambda b, c, j: (rows(b, c), j), j=j))
                for j in range(4)]
    in_specs += [pl.BlockSpec((1, D_MODEL), lambda b, c: (0, 0))] * 3
    in_specs += [pl.BlockSpec((1, A_DV), lambda b, c: (0, 0))]
    args = [p, p, p, p, *lb_terms, gn]
    if has_init:
        in_specs.append(pl.BlockSpec((1, A_HEADS, A_DK, A_DV), lambda b, c: (b, 0, 0, 0)))
        args.append(s0)
    return pl.pallas_call(
        functools.partial(_hgrn_body, chunk=chunk, has_init=has_init),
        grid=(n_seq, n_chunks),
        in_specs=in_specs,
        out_specs=[
            pl.BlockSpec((chunk, D_MODEL), lambda b, c: (b * n_chunks + c, 0)),
            pl.BlockSpec((1, A_HEADS, A_DK, A_DV), lambda b, c: (b, 0, 0, 0)),
        ],
        out_shape=[
            jax.ShapeDtypeStruct((n_seq * seq_len, D_MODEL), BF16),
            jax.ShapeDtypeStruct((n_seq, A_HEADS, A_DK, A_DV), F32),
        ],
        scratch_shapes=[pltpu.VMEM((A_HEADS, A_DV, A_DK), F32)],
        compiler_params=_cparams("parallel", "arbitrary"),
        name=name,
    )(*args)


def _softmax_sink_pv(s_parts, v_parts, sink):
    m = sink
    for s in s_parts:
        m = jnp.maximum(m, jnp.max(s, axis=-1, keepdims=True))
    denom = jnp.exp(sink - m)
    acc = None
    for s, v in zip(s_parts, v_parts):
        p = jnp.exp(s - m)
        denom = denom + jnp.sum(p, axis=-1, keepdims=True)
        pv = jnp.dot(p.astype(BF16), v, preferred_element_type=F32)
        acc = pv if acc is None else acc + pv
    return acc / denom


def _swa_prompt_body(sink_ref, q_ref, kvp_ref, kvc_ref, o_ref):
    i = pl.program_id(1)
    row = lax.broadcasted_iota(jnp.int32, (WINDOW, WINDOW), 0)
    col = lax.broadcasted_iota(jnp.int32, (WINDOW, WINDOW), 1)
    mask_prev = (col >= row) & (i > 0)
    mask_cur = col <= row
    scale = B_HEAD_DIM ** -0.5
    kv_w = B_KV_HEADS * B_HEAD_DIM
    outs = []
    for kh in range(B_KV_HEADS):
        ksl = slice(kh * B_HEAD_DIM, (kh + 1) * B_HEAD_DIM)
        vsl = slice(kv_w + kh * B_HEAD_DIM, kv_w + (kh + 1) * B_HEAD_DIM)
        k_prev = kvp_ref[:, ksl].astype(BF16)
        k_cur = kvc_ref[:, ksl].astype(BF16)
        v_prev = kvp_ref[:, vsl].astype(BF16)
        v_cur = kvc_ref[:, vsl].astype(BF16)
        for gi in range(B_GROUP):
            h = kh * B_GROUP + gi
            qh = q_ref[:, h * B_HEAD_DIM:(h + 1) * B_HEAD_DIM].astype(BF16)
            dn = (((1,), (1,)), ((), ()))
            s_prev = lax.dot_general(qh, k_prev, dn, preferred_element_type=F32) * scale
            s_cur = lax.dot_general(qh, k_cur, dn, preferred_element_type=F32) * scale
            s_prev = jnp.where(mask_prev, s_prev, NEG_INF)
            s_cur = jnp.where(mask_cur, s_cur, NEG_INF)
            outs.append(_softmax_sink_pv([s_prev, s_cur], [v_prev, v_cur], sink_ref[h]))
    o_ref[...] = jnp.concatenate(outs, axis=-1).astype(o_ref.dtype)


def _swa_prompt(q, kv, sinks, *, n_seq, seq_len):
    nb = seq_len // WINDOW
    kv_cols = kv.shape[1]
    return pl.pallas_call(
        _swa_prompt_body,
        grid=(n_seq, nb),
        in_specs=[
            pl.BlockSpec(memory_space=pltpu.SMEM),
            pl.BlockSpec((WINDOW, D_MODEL), lambda b, i: (b * nb + i, 0)),
            pl.BlockSpec((WINDOW, kv_cols), lambda b, i: (b * nb + jnp.maximum(i - 1, 0), 0)),
            pl.BlockSpec((WINDOW, kv_cols), lambda b, i: (b * nb + i, 0)),
        ],
        out_specs=pl.BlockSpec((WINDOW, D_MODEL), lambda b, i: (b * nb + i, 0)),
        out_shape=jax.ShapeDtypeStruct((n_seq * seq_len, D_MODEL), BF16),
        compiler_params=_cparams("parallel", "arbitrary"),
        name="swa_prompt",
    )(sinks, q, kv, kv)


def _swa_sample_body(sink_ref, q_ref, kc_ref, vc_ref, kvn_ref, o_ref, *, t_new):
    rows = B_GROUP * t_new
    t_row = lax.broadcasted_iota(jnp.int32, (rows, WINDOW), 0) % t_new
    c_col = lax.broadcasted_iota(jnp.int32, (rows, WINDOW), 1)
    mask_cache = c_col >= t_row
    t_row_n = lax.broadcasted_iota(jnp.int32, (rows, t_new), 0) % t_new
    t_col_n = lax.broadcasted_iota(jnp.int32, (rows, t_new), 1)
    mask_new = t_col_n <= t_row_n
    scale = B_HEAD_DIM ** -0.5
    kv_w = B_KV_HEADS * B_HEAD_DIM
    dn = (((1,), (1,)), ((), ()))
    for s in range(SAMPLE_SEQS_PER_STEP):
        for kh in range(B_KV_HEADS):
            ksl = slice(kh * B_HEAD_DIM, (kh + 1) * B_HEAD_DIM)
            q = q_ref[s, kh].astype(BF16)
            k_c = kc_ref[s, :, ksl].astype(BF16)
            v_c = vc_ref[s, :, ksl].astype(BF16)
            k_n = kvn_ref[s, :, ksl].astype(BF16)
            v_n = kvn_ref[s, :, kv_w + kh * B_HEAD_DIM:kv_w + (kh + 1) * B_HEAD_DIM].astype(BF16)
            s_c = lax.dot_general(q, k_c, dn, preferred_element_type=F32) * scale
            s_n = lax.dot_general(q, k_n, dn, preferred_element_type=F32) * scale
            s_c = jnp.where(mask_cache, s_c, NEG_INF)
            s_n = jnp.where(mask_new, s_n, NEG_INF)
            o = _softmax_sink_pv([s_c, s_n], [v_c, v_n], sink_ref[kh])
            o_ref[s, kh] = o.astype(o_ref.dtype)


def _swa_sample(q, cache_k, cache_v, kv_new, sink_rows, *, t_new):
    n_seq = q.shape[0]
    rows = B_GROUP * t_new
    sb = SAMPLE_SEQS_PER_STEP
    kv_w = B_KV_HEADS * B_HEAD_DIM
    return pl.pallas_call(
        functools.partial(_swa_sample_body, t_new=t_new),
        grid=(n_seq // sb,),
        in_specs=[
            pl.BlockSpec((B_KV_HEADS, rows, 1), lambda b: (0, 0, 0)),
            pl.BlockSpec((sb, B_KV_HEADS, rows, B_HEAD_DIM), lambda b: (b, 0, 0, 0)),
            pl.BlockSpec((sb, WINDOW, kv_w), lambda b: (b, 0, 0)),
            pl.BlockSpec((sb, WINDOW, kv_w), lambda b: (b, 0, 0)),
            pl.BlockSpec((sb, t_new, 2 * kv_w), lambda b: (b, 0, 0)),
        ],
        out_specs=pl.BlockSpec((sb, B_KV_HEADS, rows, B_HEAD_DIM), lambda b: (b, 0, 0, 0)),
        out_shape=jax.ShapeDtypeStruct((n_seq, B_KV_HEADS, rows, B_HEAD_DIM), BF16),
        compiler_params=_cparams("parallel"),
        name="swa_sample",
    )(sink_rows, q, cache_k, cache_v, kv_new)


def kernel(x_prompt, x_sample, state_hgrn, cache_k_window, cache_v_window, norm_g, ffn_w_in, ffn_w_out,
           a_w_in, a_lower_bounds, a_gnorm, a_w_out, kv_norm, kv_w, kv_b, b_w_q, b_b_q, b_sinks, b_w_o, b_b_o):
    n_pb, p_len, _ = x_prompt.shape
    n_sb, s_len, _ = x_sample.shape
    n_p = n_pb * p_len
    n_s = n_sb * s_len
    kv_cols = B_KV_HEADS * B_HEAD_DIM

    x = jnp.concatenate([x_prompt.reshape(n_p, D_MODEL), x_sample.reshape(n_s, D_MODEL)], axis=0)

    ffn_w_in_b = ffn_w_in.astype(BF16)
    ffn_w_out_b = ffn_w_out.astype(BF16)
    a_w_in_b = a_w_in.astype(BF16)
    a_w_out_b = a_w_out.astype(BF16)
    kv_w_b = kv_w.astype(BF16)
    b_w_q_b = b_w_q.astype(BF16)
    b_w_o_b = b_w_o.astype(BF16)
    zero_bias = jnp.zeros((1, D_MODEL), F32)

    lb_soft = jax.nn.softmax(a_lower_bounds.astype(F32), axis=0)
    lbs = jnp.cumsum(lb_soft, axis=0) - lb_soft[0]
    pos = lbs > 0
    log_lb = jnp.where(pos, jnp.log(jnp.where(pos, lbs, 1.0)), NEG_INF)
    log1m_lb = jnp.log1p(-lbs)
    onem_lb = 1.0 - lbs

    cache_k = cache_k_window.reshape(n_sb, WINDOW, kv_cols)
    cache_v = cache_v_window.reshape(n_sb, WINDOW, kv_cols)

    def g_row(l, j):
        return norm_g[l, j].reshape(1, D_MODEL)

    s_prompt, s_sample = [], []
    kv = None
    for l in range(DEPTH):
        x = _ffn(x, g_row(l, 0), g_row(l, 1), ffn_w_in_b[l, 0], ffn_w_out_b[l, 0])
        if l < N_A_LAYERS:
            p = _pre_mm(x, g_row(l, 2), a_w_in_b[l], jnp.zeros((1, 4 * D_MODEL), F32), "hgrn_in")
            lb_terms = (log_lb[l].reshape(1, D_MODEL), log1m_lb[l].reshape(1, D_MODEL),
                        onem_lb[l].reshape(1, D_MODEL))
            gn = a_gnorm[l].reshape(1, A_DV)
            o_p, s_p = _hgrn_scan(p, lb_terms, gn, None, row0=0, n_seq=n_pb, seq_len=p_len,
                                  chunk=math.gcd(p_len, PROMPT_CHUNK), name="hgrn_prompt")
            o_s, s_s = _hgrn_scan(p, lb_terms, gn, state_hgrn[l], row0=n_p, n_seq=n_sb, seq_len=s_len,
                                  chunk=s_len, name="hgrn_sample")
            s_prompt.append(s_p)
            s_sample.append(s_s)
            o = jnp.concatenate([o_p, o_s], axis=0)
            x = _post_mm(o, a_w_out_b[l], zero_bias, g_row(l, 3), x, "hgrn_out")
        else:
            j = l - N_A_LAYERS
            q = _pre_mm(x, g_row(l, 2), b_w_q_b[j], b_b_q[j].reshape(1, D_MODEL), "swa_q")
            o_p = _swa_prompt(q, kv, b_sinks[j], n_seq=n_pb, seq_len=p_len)
            q_s = q[n_p:].reshape(n_sb, s_len, B_KV_HEADS, B_GROUP, B_HEAD_DIM)
            q_s = q_s.transpose(0, 2, 3, 1, 4).reshape(n_sb, B_KV_HEADS, B_GROUP * s_len, B_HEAD_DIM)
            sink_rows = jnp.repeat(b_sinks[j].reshape(B_KV_HEADS, B_GROUP), s_len, axis=1)[..., None]
            o_s = _swa_sample(q_s, cache_k, cache_v, kv[n_p:].reshape(n_sb, s_len, 2 * kv_cols),
                              sink_rows, t_new=s_len)
            o_s = o_s.reshape(n_sb, B_KV_HEADS, B_GROUP, s_len, B_HEAD_DIM)
            o_s = o_s.transpose(0, 3, 1, 2, 4).reshape(n_s, D_MODEL)
            o = jnp.concatenate([o_p, o_s], axis=0)
            x = _post_mm(o, b_w_o_b[j], b_b_o[j].reshape(1, D_MODEL), g_row(l, 3), x, "swa_out")
        x = _ffn(x, g_row(l, 4), g_row(l, 5), ffn_w_in_b[l, 1], ffn_w_out_b[l, 1])
        if l == N_A_LAYERS - 1:
            kv = _pre_mm(x, kv_norm.reshape(1, D_MODEL), kv_w_b, kv_b.reshape(1, 2 * kv_cols), "shared_kv")

    y_prompt = x[:n_p].reshape(n_pb, p_len, D_MODEL)
    y_sample = x[n_p:].reshape(n_sb, s_len, D_MODEL)

    kv_p = kv[:n_p].reshape(n_pb, p_len, 2, B_KV_HEADS, B_HEAD_DIM)[:, p_len - WINDOW:]
    kv_s = kv[n_p:].reshape(n_sb, s_len, 2, B_KV_HEADS, B_HEAD_DIM)
    k_win_sample = jnp.concatenate([cache_k_window, kv_s[:, :, 0]], axis=1)[:, -WINDOW:]
    v_win_sample = jnp.concatenate([cache_v_window, kv_s[:, :, 1]], axis=1)[:, -WINDOW:]
    return (y_prompt, y_sample, jnp.stack(s_prompt), jnp.stack(s_sample),
            kv_p[:, :, 0], kv_p[:, :, 1], k_win_sample, v_win_sample)
```

```python
import functools
import math

import jax
import jax.numpy as jnp
from jax import lax
from jax.experimental import pallas as pl
from jax.experimental.pallas import tpu as pltpu

F32 = jnp.float32
BF16 = jnp.bfloat16

D_MODEL = 1024
DEPTH = 4
N_A_LAYERS = 2
A_HEADS = 8
A_DK = 128
A_DV = 128
B_HEAD_DIM = 64
B_HEADS = 16
B_KV_HEADS = 2
B_GROUP = 8
WINDOW = 128
D_FF = 2816
EPS = 1e-6
NEG_INF = -1e30

VMEM_LIMIT_BYTES = 56 * 1024 * 1024

ROW_TILE = 1024
FFN_TILE = 256
FFN_ROW_SPLIT = 2
PROMPT_CHUNK = 64
SAMPLE_SEQS_PER_STEP = 8
SAMPLE_SEQS_PER_SCAN_STEP = 4


def _rms(x, g):
    return x * lax.rsqrt(jnp.mean(x * x, axis=-1, keepdims=True) + EPS) * g


def _sigmoid(x):
    return 1.0 / (1.0 + jnp.exp(-x))


def _cparams(*sem):
    return pltpu.CompilerParams(dimension_semantics=sem, vmem_limit_bytes=VMEM_LIMIT_BYTES)


def _ffn_body(x_ref, gpre_ref, gpost_ref, wg_ref, wu_ref, wo_ref, out_ref, h_ref, acc_ref, *, nf):
    f = pl.program_id(1)
    rows = ROW_TILE // FFN_ROW_SPLIT

    def step(first, last):
        for r in range(FFN_ROW_SPLIT):
            rs = slice(r * rows, (r + 1) * rows)
            if first:
                h = _rms(x_ref[rs, :], gpre_ref[...]).astype(BF16)
                h_ref[rs, :] = h
            else:
                h = h_ref[rs, :]
            gate = jnp.dot(h, wg_ref[...], preferred_element_type=F32)
            up = jnp.dot(h, wu_ref[...], preferred_element_type=F32)
            act = (gate * _sigmoid(gate) * up).astype(BF16)
            part = jnp.dot(act, wo_ref[...], preferred_element_type=F32)
            if first:
                acc_ref[rs, :] = part
            elif last:
                out_ref[rs, :] = x_ref[rs, :] + 0.5 * _rms(acc_ref[rs, :] + part, gpost_ref[...])
            else:
                acc_ref[rs, :] += part

    pl.when(f == 0)(functools.partial(step, True, False))
    pl.when((f > 0) & (f < nf - 1))(functools.partial(step, False, False))
    pl.when(f == nf - 1)(functools.partial(step, False, True))


def _ffn(x, g_pre, g_post, w_in, w_out, layer, which):
    n = x.shape[0]
    nf = D_FF // FFN_TILE
    assert nf >= 2
    return pl.pallas_call(
        functools.partial(_ffn_body, nf=nf),
        grid=(n // ROW_TILE, nf),
        in_specs=[
            pl.BlockSpec((ROW_TILE, D_MODEL), lambda i, f: (i, 0)),
            pl.BlockSpec((1, D_MODEL), lambda i, f: (0, 0)),
            pl.BlockSpec((1, D_MODEL), lambda i, f: (0, 0)),
            pl.BlockSpec((None, None, D_MODEL, FFN_TILE), lambda i, f: (layer, which, 0, f)),
            pl.BlockSpec((None, None, D_MODEL, FFN_TILE), lambda i, f: (layer, which, 0, nf + f)),
            pl.BlockSpec((None, None, FFN_TILE, D_MODEL), lambda i, f: (layer, which, f, 0)),
        ],
        out_specs=pl.BlockSpec((ROW_TILE, D_MODEL), lambda i, f: (i, 0)),
        out_shape=jax.ShapeDtypeStruct((n, D_MODEL), F32),
        scratch_shapes=[pltpu.VMEM((ROW_TILE, D_MODEL), BF16), pltpu.VMEM((ROW_TILE, D_MODEL), F32)],
        compiler_params=_cparams("parallel", "arbitrary"),
        name="ffn",
    )(x, g_pre, g_post, w_in, w_in, w_out)


def _pre_mm_body(x_ref, g_ref, w_ref, b_ref, out_ref, h_ref):
    @pl.when(pl.program_id(1) == 0)
    def _():
        h_ref[...] = _rms(x_ref[...], g_ref[...]).astype(BF16)

    out_ref[...] = jnp.dot(h_ref[...], w_ref[...], preferred_element_type=F32) + b_ref[...]


def _pre_mm(x, g, w, b, name):
    n = x.shape[0]
    n_out = w.shape[1]
    tn = min(n_out, 1024)
    return pl.pallas_call(
        _pre_mm_body,
        grid=(n // ROW_TILE, n_out // tn),
        in_specs=[
            pl.BlockSpec((ROW_TILE, D_MODEL), lambda i, j: (i, 0)),
            pl.BlockSpec((1, D_MODEL), lambda i, j: (0, 0)),
            pl.BlockSpec((D_MODEL, tn), lambda i, j: (0, j)),
            pl.BlockSpec((1, tn), lambda i, j: (0, j)),
        ],
        out_specs=pl.BlockSpec((ROW_TILE, tn), lambda i, j: (i, j)),
        out_shape=jax.ShapeDtypeStruct((n, n_out), F32),
        scratch_shapes=[pltpu.VMEM((ROW_TILE, D_MODEL), BF16)],
        compiler_params=_cparams("parallel", "arbitrary"),
        name=name,
    )(x, g, w, b)


def _post_mm_body(o_ref, w_ref, b_ref, g_ref, x_ref, out_ref):
    m = jnp.dot(o_ref[...], w_ref[...], preferred_element_type=F32) + b_ref[...]
    out_ref[...] = x_ref[...] + _rms(m, g_ref[...])


def _post_mm(o, w, b, g, x, name):
    n = x.shape[0]
    return pl.pallas_call(
        _post_mm_body,
        grid=(n // ROW_TILE,),
        in_specs=[
            pl.BlockSpec((ROW_TILE, D_MODEL), lambda i: (i, 0)),
            pl.BlockSpec((D_MODEL, D_MODEL), lambda i: (0, 0)),
            pl.BlockSpec((1, D_MODEL), lambda i: (0, 0)),
            pl.BlockSpec((1, D_MODEL), lambda i: (0, 0)),
            pl.BlockSpec((ROW_TILE, D_MODEL), lambda i: (i, 0)),
        ],
        out_specs=pl.BlockSpec((ROW_TILE, D_MODEL), lambda i: (i, 0)),
        out_shape=jax.ShapeDtypeStruct((n, D_MODEL), F32),
        compiler_params=_cparams("parallel"),
        name=name,
    )(o, w, b, g, x)


def _split3(a):
    a1 = a.astype(BF16)
    r1 = a - a1.astype(F32)
    a2 = r1.astype(BF16)
    a3 = (r1 - a2.astype(F32)).astype(BF16)
    return a1, a2, a3


def _hgrn_body(*refs, chunk, seqs, has_init, has_prev_out):
    refs = list(refs)
    q_ref, z_ref, i_ref, g_ref, loglb_ref, log1mlb_ref, onemlb_ref, gn_ref = refs[:8]
    del refs[:8]
    s0_ref = refs.pop(0) if has_init else None
    if has_prev_out:
        refs.pop(0)
    o_ref, sout_ref, st_ref = refs
    c = pl.program_id(1)
    n_chunks = pl.num_programs(1)

    @pl.when(c == 0)
    def _():
        if has_init:
            for s in range(seqs):
                for h in range(A_HEADS):
                    st_ref[s * A_HEADS + h] = s0_ref[s, h].T
        else:
            st_ref[...] = jnp.zeros_like(st_ref)

    row = lax.broadcasted_iota(jnp.int32, (chunk, chunk), 0)
    col = lax.broadcasted_iota(jnp.int32, (chunk, chunk), 1)
    causal = col <= row
    tril = causal.astype(BF16)
    mid = chunk // 2
    gn = gn_ref[...]
    log_lb = loglb_ref[...]
    log1m_lb = log1mlb_ref[...]
    onem_lb = onemlb_ref[...]
    dn_t = (((1,), (1,)), ((), ()))
    dn_0 = (((0,), (0,)), ((), ()))

    for s in range(seqs):
        rs = slice(s * chunk, (s + 1) * chunk)
        q = q_ref[rs, :]
        z = z_ref[rs, :]
        g = g_ref[rs, :]
        vb = i_ref[rs, :].astype(BF16)

        e = jnp.exp(-jnp.abs(z))
        r = 1.0 / (1.0 + e)
        log_sig = jnp.minimum(z, 0.0) - jnp.log1p(e)
        b_ = log1m_lb + log_sig
        log_f = jnp.maximum(log_lb, b_) + jnp.log1p(jnp.exp(-jnp.abs(log_lb - b_)))
        k = onem_lb * jnp.where(z >= 0.0, e * r, r)

        f1, f2, f3 = _split3(log_f)
        cum = (jnp.dot(tril, f1, preferred_element_type=F32)
               + jnp.dot(tril, f2, preferred_element_type=F32)
               + jnp.dot(tril, f3, preferred_element_type=F32))
        last = cum[chunk - 1:chunk, :]
        ref_row = cum[mid:mid + 1, :]

        qs = (q * jnp.exp(cum - ref_row)).astype(BF16)
        ks = (k * jnp.exp(ref_row - cum)).astype(BF16)
        qe = (q * jnp.exp(cum)).astype(BF16)
        ke = (k * jnp.exp(last - cum)).astype(BF16)
        e_last = jnp.exp(last)

        outs = []
        for h in range(A_HEADS):
            sl = slice(h * A_DK, (h + 1) * A_DK)
            scores = lax.dot_general(qs[:, sl], ks[:, sl], dn_t, preferred_element_type=F32)
            scores = jnp.where(causal, scores, 0.0).astype(BF16)
            st = st_ref[s * A_HEADS + h]
            o = (jnp.dot(scores, vb[:, sl], preferred_element_type=F32)
                 + lax.dot_general(qe[:, sl], st.astype(BF16), dn_t, preferred_element_type=F32))
            st_ref[s * A_HEADS + h] = (e_last[:, sl] * st
                                       + lax.dot_general(vb[:, sl], ke[:, sl], dn_0,
                                                         preferred_element_type=F32))
            outs.append(_rms(o, gn))
        o_ref[rs, :] = (jnp.concatenate(outs, axis=-1) * (g * _sigmoid(g))).astype(o_ref.dtype)

    @pl.when(c == n_chunks - 1)
    def _():
        for s in range(seqs):
            for h in range(A_HEADS):
                sout_ref[s, h] = st_ref[s * A_HEADS + h].T


def _hgrn_scan(p, lb_terms, gn, s0, s_prev, *, layer, row0, n_seq, seq_len, chunk, seqs, out_rows, name):
    n_chunks = seq_len // chunk
    blk = seqs * chunk
    assert n_chunks == 1 or seqs == 1
    blk0 = row0 // blk
    has_init = s0 is not None
    has_prev = s_prev is not None

    in_specs = [pl.BlockSpec((blk, D_MODEL), functools.partial(lambda b, c, j: (blk0 + b * n_chunks + c, j), j=j))
                for j in range(4)]
    in_specs += [pl.BlockSpec((1, D_MODEL), lambda b, c: (0, 0))] * 3
    in_specs += [pl.BlockSpec((1, A_DV), lambda b, c: (0, 0))]
    args = [p, p, p, p, *lb_terms, gn]
    if has_init:
        in_specs.append(pl.BlockSpec((None, seqs, A_HEADS, A_DK, A_DV), lambda b, c: (layer, b, 0, 0, 0)))
        args.append(s0)
    aliases = {}
    if has_prev:
        in_specs.append(pl.BlockSpec(memory_space=pl.ANY))
        aliases = {len(args): 1}
        args.append(s_prev)
    return pl.pallas_call(
        functools.partial(_hgrn_body, chunk=chunk, seqs=seqs, has_init=has_init, has_prev_out=has_prev),
        grid=(n_seq // seqs, n_chunks),
        in_specs=in_specs,
        out_specs=[
            pl.BlockSpec((blk, D_MODEL), lambda b, c: (b * n_chunks + c, 0)),
            pl.BlockSpec((None, seqs, A_HEADS, A_DK, A_DV), lambda b, c: (layer, b, 0, 0, 0)),
        ],
        out_shape=[
            jax.ShapeDtypeStruct((out_rows, D_MODEL), BF16),
            jax.ShapeDtypeStruct((N_A_LAYERS, n_seq, A_HEADS, A_DK, A_DV), F32),
        ],
        scratch_shapes=[pltpu.VMEM((seqs * A_HEADS, A_DV, A_DK), F32)],
        input_output_aliases=aliases,
        compiler_params=_cparams("parallel", "arbitrary"),
        name=name,
    )(*args)


def _softmax_sink_pv(s_parts, v_parts, sink):
    m = sink
    for s in s_parts:
        m = jnp.maximum(m, jnp.max(s, axis=-1, keepdims=True))
    denom = jnp.exp(sink - m)
    acc = None
    for s, v in zip(s_parts, v_parts):
        p = jnp.exp(s - m)
        denom = denom + jnp.sum(p, axis=-1, keepdims=True)
        pv = jnp.dot(p.astype(BF16), v, preferred_element_type=F32)
        acc = pv if acc is None else acc + pv
    return acc / denom


def _swa_prompt_body(sink_ref, q_ref, kvp_ref, kvc_ref, o_ref):
    i = pl.program_id(1)
    key = lax.broadcasted_iota(jnp.int32, (2 * WINDOW, WINDOW), 0)
    qry = lax.broadcasted_iota(jnp.int32, (2 * WINDOW, WINDOW), 1)
    mask = ((key < WINDOW) & (key >= qry) & (i > 0)) | ((key >= WINDOW) & (key - WINDOW <= qry))
    scale = B_HEAD_DIM ** -0.5
    kv_w = B_KV_HEADS * B_HEAD_DIM
    dn_t = (((1,), (1,)), ((), ()))
    dn_0 = (((0,), (0,)), ((), ()))
    outs = []
    for kh in range(B_KV_HEADS):
        ksl = slice(kh * B_HEAD_DIM, (kh + 1) * B_HEAD_DIM)
        vsl = slice(kv_w + kh * B_HEAD_DIM, kv_w + (kh + 1) * B_HEAD_DIM)
        k = jnp.concatenate([kvp_ref[:, ksl], kvc_ref[:, ksl]], axis=0).astype(BF16)
        v = jnp.concatenate([kvp_ref[:, vsl], kvc_ref[:, vsl]], axis=0).astype(BF16)
        heads = range(kh * B_GROUP, (kh + 1) * B_GROUP)
        q = jnp.concatenate([q_ref[:, h * B_HEAD_DIM:(h + 1) * B_HEAD_DIM] for h in heads], axis=0)
        q = (q * scale).astype(BF16)
        s_t = lax.dot_general(k, q, dn_t, preferred_element_type=F32)
        p_parts, inv_parts = [], []
        for gi, h in enumerate(heads):
            s = jnp.where(mask, s_t[:, gi * WINDOW:(gi + 1) * WINDOW], NEG_INF)
            sink = sink_ref[h]
            m = jnp.maximum(jnp.max(s, axis=0, keepdims=True), sink)
            p = jnp.exp(s - m)
            inv_parts.append(1.0 / (jnp.sum(p, axis=0, keepdims=True) + jnp.exp(sink - m)))
            p_parts.append(p.astype(BF16))
        p_t = jnp.concatenate(p_parts, axis=1)
        o_t = lax.dot_general(v, p_t, dn_0, preferred_element_type=F32)
        o_t = o_t * jnp.concatenate(inv_parts, axis=1)
        outs += [o_t[:, gi * WINDOW:(gi + 1) * WINDOW] for gi in range(B_GROUP)]
    o_ref[...] = jnp.concatenate(outs, axis=0).T.astype(o_ref.dtype)


def _swa_prompt(q, kv, sinks, *, n_seq, seq_len, out_rows):
    nb = seq_len // WINDOW
    kv_cols = kv.shape[1]
    return pl.pallas_call(
        _swa_prompt_body,
        grid=(n_seq, nb),
        in_specs=[
            pl.BlockSpec(memory_space=pltpu.SMEM),
            pl.BlockSpec((WINDOW, D_MODEL), lambda b, i: (b * nb + i, 0)),
            pl.BlockSpec((WINDOW, kv_cols), lambda b, i: (b * nb + jnp.maximum(i - 1, 0), 0)),
            pl.BlockSpec((WINDOW, kv_cols), lambda b, i: (b * nb + i, 0)),
        ],
        out_specs=pl.BlockSpec((WINDOW, D_MODEL), lambda b, i: (b * nb + i, 0)),
        out_shape=jax.ShapeDtypeStruct((out_rows, D_MODEL), BF16),
        compiler_params=_cparams("parallel", "arbitrary"),
        name="swa_prompt",
    )(sinks, q, kv, kv)


def _swa_sample_body(sink_ref, q_ref, kc_ref, vc_ref, kvn_ref, o_ref, *, t_new):
    rows = B_GROUP * t_new
    t_row = lax.broadcasted_iota(jnp.int32, (rows, WINDOW), 0) % t_new
    c_col = lax.broadcasted_iota(jnp.int32, (rows, WINDOW), 1)
    mask_cache = c_col >= t_row
    t_row_n = lax.broadcasted_iota(jnp.int32, (rows, t_new), 0) % t_new
    t_col_n = lax.broadcasted_iota(jnp.int32, (rows, t_new), 1)
    mask_new = t_col_n <= t_row_n
    scale = B_HEAD_DIM ** -0.5
    kv_w = B_KV_HEADS * B_HEAD_DIM
    dn = (((1,), (1,)), ((), ()))
    for s in range(SAMPLE_SEQS_PER_STEP):
        for kh in range(B_KV_HEADS):
            ksl = slice(kh * B_HEAD_DIM, (kh + 1) * B_HEAD_DIM)
            q = q_ref[s, kh].astype(BF16)
            k_c = kc_ref[s, :, ksl].astype(BF16)
            v_c = vc_ref[s, :, ksl].astype(BF16)
            k_n = kvn_ref[s, :, ksl].astype(BF16)
            v_n = kvn_ref[s, :, kv_w + kh * B_HEAD_DIM:kv_w + (kh + 1) * B_HEAD_DIM].astype(BF16)
            s_c = lax.dot_general(q, k_c, dn, preferred_element_type=F32) * scale
            s_n = lax.dot_general(q, k_n, dn, preferred_element_type=F32) * scale
            s_c = jnp.where(mask_cache, s_c, NEG_INF)
            s_n = jnp.where(mask_new, s_n, NEG_INF)
            o = _softmax_sink_pv([s_c, s_n], [v_c, v_n], sink_ref[kh])
            o_ref[s, kh] = o.astype(o_ref.dtype)


def _swa_sample(q, cache_k, cache_v, kv_new, sink_rows, *, t_new):
    n_seq = q.shape[0]
    rows = B_GROUP * t_new
    sb = SAMPLE_SEQS_PER_STEP
    kv_w = B_KV_HEADS * B_HEAD_DIM
    return pl.pallas_call(
        functools.partial(_swa_sample_body, t_new=t_new),
        grid=(n_seq // sb,),
        in_specs=[
            pl.BlockSpec((B_KV_HEADS, rows, 1), lambda b: (0, 0, 0)),
            pl.BlockSpec((sb, B_KV_HEADS, rows, B_HEAD_DIM), lambda b: (b, 0, 0, 0)),
            pl.BlockSpec((sb, WINDOW, kv_w), lambda b: (b, 0, 0)),
            pl.BlockSpec((sb, WINDOW, kv_w), lambda b: (b, 0, 0)),
            pl.BlockSpec((sb, t_new, 2 * kv_w), lambda b: (b, 0, 0)),
        ],
        out_specs=pl.BlockSpec((sb, B_KV_HEADS, rows, B_HEAD_DIM), lambda b: (b, 0, 0, 0)),
        out_shape=jax.ShapeDtypeStruct((n_seq, B_KV_HEADS, rows, B_HEAD_DIM), BF16),
        compiler_params=_cparams("parallel"),
        name="swa_sample",
    )(sink_rows, q, cache_k, cache_v, kv_new)


def kernel(x_prompt, x_sample, state_hgrn, cache_k_window, cache_v_window, norm_g, ffn_w_in, ffn_w_out,
           a_w_in, a_lower_bounds, a_gnorm, a_w_out, kv_norm, kv_w, kv_b, b_w_q, b_b_q, b_sinks, b_w_o, b_b_o):
    n_pb, p_len, _ = x_prompt.shape
    n_sb, s_len, _ = x_sample.shape
    n_p = n_pb * p_len
    n_s = n_sb * s_len
    kv_cols = B_KV_HEADS * B_HEAD_DIM

    x = jnp.concatenate([x_prompt.reshape(n_p, D_MODEL), x_sample.reshape(n_s, D_MODEL)], axis=0)

    ffn_w_in_b = ffn_w_in.astype(BF16)
    ffn_w_out_b = ffn_w_out.astype(BF16)
    a_w_in_b = a_w_in.astype(BF16)
    a_w_out_b = a_w_out.astype(BF16)
    kv_w_b = kv_w.astype(BF16)
    b_w_q_b = b_w_q.astype(BF16)
    b_w_o_b = b_w_o.astype(BF16)
    zero_bias = jnp.zeros((1, D_MODEL), F32)

    lb_soft = jax.nn.softmax(a_lower_bounds.astype(F32), axis=0)
    lbs = jnp.cumsum(lb_soft, axis=0) - lb_soft[0]
    pos = lbs > 0
    log_lb = jnp.where(pos, jnp.log(jnp.where(pos, lbs, 1.0)), NEG_INF)
    log1m_lb = jnp.log1p(-lbs)
    onem_lb = 1.0 - lbs

    cache_k = cache_k_window.reshape(n_sb, WINDOW, kv_cols)
    cache_v = cache_v_window.reshape(n_sb, WINDOW, kv_cols)

    def g_row(l, j):
        return norm_g[l, j].reshape(1, D_MODEL)

    s_prompt = s_sample = None
    kv = None
    for l in range(DEPTH):
        x = _ffn(x, g_row(l, 0), g_row(l, 1), ffn_w_in_b, ffn_w_out_b, l, 0)
        if l < N_A_LAYERS:
            p = _pre_mm(x, g_row(l, 2), a_w_in_b[l], jnp.zeros((1, 4 * D_MODEL), F32), "hgrn_in")
            lb_terms = (log_lb[l].reshape(1, D_MODEL), log1m_lb[l].reshape(1, D_MODEL),
                        onem_lb[l].reshape(1, D_MODEL))
            gn = a_gnorm[l].reshape(1, A_DV)
            o_p, s_prompt = _hgrn_scan(p, lb_terms, gn, None, s_prompt, layer=l, row0=0, n_seq=n_pb,
                                       seq_len=p_len, chunk=math.gcd(p_len, PROMPT_CHUNK), seqs=1,
                                       out_rows=n_p + n_s, name="hgrn_prompt")
            o_s, s_sample = _hgrn_scan(p, lb_terms, gn, state_hgrn, s_sample, layer=l, row0=n_p, n_seq=n_sb,
                                       seq_len=s_len, chunk=s_len, seqs=SAMPLE_SEQS_PER_SCAN_STEP,
                                       out_rows=n_s, name="hgrn_sample")
            o = lax.dynamic_update_slice(o_p, o_s, (n_p, 0))
            x = _post_mm(o, a_w_out_b[l], zero_bias, g_row(l, 3), x, "hgrn_out")
        else:
            j = l - N_A_LAYERS
            q = _pre_mm(x, g_row(l, 2), b_w_q_b[j], b_b_q[j].reshape(1, D_MODEL), "swa_q")
            o_p = _swa_prompt(q, kv, b_sinks[j], n_seq=n_pb, seq_len=p_len, out_rows=n_p + n_s)
            q_s = q[n_p:].reshape(n_sb, s_len, B_KV_HEADS, B_GROUP, B_HEAD_DIM)
            q_s = q_s.transpose(0, 2, 3, 1, 4).reshape(n_sb, B_KV_HEADS, B_GROUP * s_len, B_HEAD_DIM)
            sink_rows = jnp.repeat(b_sinks[j].reshape(B_KV_HEADS, B_GROUP), s_len, axis=1)[..., None]
            o_s = _swa_sample(q_s, cache_k, cache_v, kv[n_p:].reshape(n_sb, s_len, 2 * kv_cols),
                              sink_rows, t_new=s_len)
            o_s = o_s.reshape(n_sb, B_KV_HEADS, B_GROUP, s_len, B_HEAD_DIM)
            o_s = o_s.transpose(0, 3, 1, 2, 4).reshape(n_s, D_MODEL)
            o = lax.dynamic_update_slice(o_p, o_s, (n_p, 0))
            x = _post_mm(o, b_w_o_b[j], b_b_o[j].reshape(1, D_MODEL), g_row(l, 3), x, "swa_out")
        x = _ffn(x, g_row(l, 4), g_row(l, 5), ffn_w_in_b, ffn_w_out_b, l, 1)
        if l == N_A_LAYERS - 1:
            kv = _pre_mm(x, kv_norm.reshape(1, D_MODEL), kv_w_b, kv_b.reshape(1, 2 * kv_cols), "shared_kv")

    y_prompt = x[:n_p].reshape(n_pb, p_len, D_MODEL)
    y_sample = x[n_p:].reshape(n_sb, s_len, D_MODEL)

    kv_p = jnp.stack([kv[(b + 1) * p_len - WINDOW:(b + 1) * p_len] for b in range(n_pb)])
    kv_p = kv_p.reshape(n_pb, WINDOW, 2, B_KV_HEADS, B_HEAD_DIM)
    kv_s = kv[n_p:].reshape(n_sb, s_len, 2, B_KV_HEADS, B_HEAD_DIM)
    k_win_sample = jnp.concatenate([cache_k_window, kv_s[:, :, 0]], axis=1)[:, -WINDOW:]
    v_win_sample = jnp.concatenate([cache_v_window, kv_s[:, :, 1]], axis=1)[:, -WINDOW:]
    return (y_prompt, y_sample, s_prompt, s_sample,
            kv_p[:, :, 0], kv_p[:, :, 1], k_win_sample, v_win_sample)
```

```python
import functools
import math

import jax
import jax.numpy as jnp
from jax import lax
from jax.experimental import pallas as pl
from jax.experimental.pallas import tpu as pltpu

F32 = jnp.float32
BF16 = jnp.bfloat16

D_MODEL = 1024
DEPTH = 4
N_A_LAYERS = 2
A_HEADS = 8
A_DK = 128
A_DV = 128
B_HEAD_DIM = 64
B_HEADS = 16
B_KV_HEADS = 2
B_GROUP = 8
WINDOW = 128
D_FF = 2816
EPS = 1e-6
NEG_INF = -1e30

VMEM_LIMIT_BYTES = 56 * 1024 * 1024

ROW_TILE = 1024
FFN_TILE = 256
FFN_ROW_SPLIT = 2
PROMPT_CHUNK = 64
PROMPT_CHUNKS_PER_STEP = 8
SAMPLE_SEQS_PER_STEP = 8
SAMPLE_SEQS_PER_SCAN_STEP = 4


def _rms(x, g):
    return x * lax.rsqrt(jnp.mean(x * x, axis=-1, keepdims=True) + EPS) * g


def _sigmoid(x):
    return 1.0 / (1.0 + jnp.exp(-x))


def _cparams(*sem):
    return pltpu.CompilerParams(dimension_semantics=sem, vmem_limit_bytes=VMEM_LIMIT_BYTES)


def _ffn_body(x_ref, gpre_ref, gpost_ref, win_ref, wout_ref, out_ref, h_ref, acc_ref, *, unroll):
    nf = D_FF // FFN_TILE
    rows = ROW_TILE // FFN_ROW_SPLIT

    def chunk(f, first, last):
        c0 = f * FFN_TILE if isinstance(f, int) else pl.multiple_of(f * FFN_TILE, FFN_TILE)
        wg = win_ref[:, pl.ds(c0, FFN_TILE)]
        wu = win_ref[:, pl.ds(D_FF + c0, FFN_TILE)]
        wo = wout_ref[pl.ds(c0, FFN_TILE), :]
        for r in range(FFN_ROW_SPLIT):
            rs = slice(r * rows, (r + 1) * rows)
            if first:
                h = _rms(x_ref[rs, :], gpre_ref[...]).astype(BF16)
                h_ref[rs, :] = h
            else:
                h = h_ref[rs, :]
            gate = jnp.dot(h, wg, preferred_element_type=F32)
            up = jnp.dot(h, wu, preferred_element_type=F32)
            act = (gate * _sigmoid(gate) * up).astype(BF16)
            part = jnp.dot(act, wo, preferred_element_type=F32)
            if first:
                acc_ref[rs, :] = part
            elif last:
                out_ref[rs, :] = x_ref[rs, :] + 0.5 * _rms(acc_ref[rs, :] + part, gpost_ref[...])
            else:
                acc_ref[rs, :] += part

    chunk(0, True, False)

    def middle(f, carry):
        chunk(f, False, False)
        return carry

    lax.fori_loop(1, nf - 1, middle, 0, unroll=unroll)
    chunk(nf - 1, False, True)


def _ffn(x, g_pre, g_post, w_in, w_out, layer, which, unroll=1):
    n = x.shape[0]
    assert D_FF // FFN_TILE >= 2
    resident = pl.Buffered(1)
    return pl.pallas_call(
        functools.partial(_ffn_body, unroll=unroll),
        grid=(n // ROW_TILE,),
        in_specs=[
            pl.BlockSpec((ROW_TILE, D_MODEL), lambda i: (i, 0)),
            pl.BlockSpec((1, D_MODEL), lambda i: (0, 0)),
            pl.BlockSpec((1, D_MODEL), lambda i: (0, 0)),
            pl.BlockSpec((None, None, D_MODEL, 2 * D_FF), lambda i: (layer, which, 0, 0), pipeline_mode=resident),
            pl.BlockSpec((None, None, D_FF, D_MODEL), lambda i: (layer, which, 0, 0), pipeline_mode=resident),
        ],
        out_specs=pl.BlockSpec((ROW_TILE, D_MODEL), lambda i: (i, 0)),
        out_shape=jax.ShapeDtypeStruct((n, D_MODEL), F32),
        scratch_shapes=[pltpu.VMEM((ROW_TILE, D_MODEL), BF16), pltpu.VMEM((ROW_TILE, D_MODEL), F32)],
        compiler_params=_cparams("parallel"),
        name="ffn",
    )(x, g_pre, g_post, w_in, w_out)


def _pre_mm_body(x_ref, g_ref, w_ref, b_ref, out_ref, h_ref):
    @pl.when(pl.program_id(1) == 0)
    def _():
        h_ref[...] = _rms(x_ref[...], g_ref[...]).astype(BF16)

    out_ref[...] = jnp.dot(h_ref[...], w_ref[...], preferred_element_type=F32) + b_ref[...]


def _pre_mm(x, g, w, b, name):
    n = x.shape[0]
    n_out = w.shape[1]
    tn = min(n_out, 1024)
    return pl.pallas_call(
        _pre_mm_body,
        grid=(n // ROW_TILE, n_out // tn),
        in_specs=[
            pl.BlockSpec((ROW_TILE, D_MODEL), lambda i, j: (i, 0)),
            pl.BlockSpec((1, D_MODEL), lambda i, j: (0, 0)),
            pl.BlockSpec((D_MODEL, tn), lambda i, j: (0, j)),
            pl.BlockSpec((1, tn), lambda i, j: (0, j)),
        ],
        out_specs=pl.BlockSpec((ROW_TILE, tn), lambda i, j: (i, j)),
        out_shape=jax.ShapeDtypeStruct((n, n_out), F32),
        scratch_shapes=[pltpu.VMEM((ROW_TILE, D_MODEL), BF16)],
        compiler_params=_cparams("parallel", "arbitrary"),
        name=name,
    )(x, g, w, b)


def _hgrn_in_body(x_ref, g_ref, w_ref, z_ref, qig_ref):
    rows = ROW_TILE // 2
    d = D_MODEL
    for r in range(2):
        rs = slice(r * rows, (r + 1) * rows)
        h = _rms(x_ref[rs, :], g_ref[...]).astype(BF16)
        z_ref[rs, :] = jnp.dot(h, w_ref[:, d:2 * d], preferred_element_type=F32)
        for j, c0 in enumerate((0, 2 * d, 3 * d)):
            qig_ref[rs, j * d:(j + 1) * d] = jnp.dot(h, w_ref[:, c0:c0 + d],
                                                     preferred_element_type=F32).astype(BF16)


def _hgrn_in(x, g, w, layer):
    n = x.shape[0]
    return pl.pallas_call(
        _hgrn_in_body,
        grid=(n // ROW_TILE,),
        in_specs=[
            pl.BlockSpec((ROW_TILE, D_MODEL), lambda i: (i, 0)),
            pl.BlockSpec((1, D_MODEL), lambda i: (0, 0)),
            pl.BlockSpec((None, D_MODEL, 4 * D_MODEL), lambda i: (layer, 0, 0), pipeline_mode=pl.Buffered(1)),
        ],
        out_specs=[
            pl.BlockSpec((ROW_TILE, D_MODEL), lambda i: (i, 0)),
            pl.BlockSpec((ROW_TILE, 3 * D_MODEL), lambda i: (i, 0)),
        ],
        out_shape=[
            jax.ShapeDtypeStruct((n, D_MODEL), F32),
            jax.ShapeDtypeStruct((n, 3 * D_MODEL), BF16),
        ],
        compiler_params=_cparams("parallel"),
        name="hgrn_in",
    )(x, g, w)


def _post_mm_body(o_ref, w_ref, b_ref, g_ref, x_ref, out_ref):
    m = jnp.dot(o_ref[...], w_ref[...], preferred_element_type=F32) + b_ref[...]
    out_ref[...] = x_ref[...] + _rms(m, g_ref[...])


def _post_mm(o, w, b, g, x, name):
    n = x.shape[0]
    return pl.pallas_call(
        _post_mm_body,
        grid=(n // ROW_TILE,),
        in_specs=[
            pl.BlockSpec((ROW_TILE, D_MODEL), lambda i: (i, 0)),
            pl.BlockSpec((D_MODEL, D_MODEL), lambda i: (0, 0)),
            pl.BlockSpec((1, D_MODEL), lambda i: (0, 0)),
            pl.BlockSpec((1, D_MODEL), lambda i: (0, 0)),
            pl.BlockSpec((ROW_TILE, D_MODEL), lambda i: (i, 0)),
        ],
        out_specs=pl.BlockSpec((ROW_TILE, D_MODEL), lambda i: (i, 0)),
        out_shape=jax.ShapeDtypeStruct((n, D_MODEL), F32),
        compiler_params=_cparams("parallel"),
        name=name,
    )(o, w, b, g, x)


def _split3(a):
    a1 = a.astype(BF16)
    r1 = a - a1.astype(F32)
    a2 = r1.astype(BF16)
    a3 = (r1 - a2.astype(F32)).astype(BF16)
    return a1, a2, a3


def _hgrn_body(*refs, chunk, subs, seqs, has_init, has_prev_out):
    refs = list(refs)
    q_ref, z_ref, i_ref, g_ref, loglb_ref, log1mlb_ref, onemlb_ref, gn_ref = refs[:8]
    del refs[:8]
    s0_ref = refs.pop(0) if has_init else None
    if has_prev_out:
        refs.pop(0)
    o_ref, sout_ref, st_ref = refs
    c = pl.program_id(1)
    n_chunks = pl.num_programs(1)

    @pl.when(c == 0)
    def _():
        if has_init:
            for s in range(seqs):
                for h in range(A_HEADS):
                    st_ref[s * A_HEADS + h] = s0_ref[s, h].T
        else:
            st_ref[...] = jnp.zeros_like(st_ref)

    row = lax.broadcasted_iota(jnp.int32, (chunk, chunk), 0)
    col = lax.broadcasted_iota(jnp.int32, (chunk, chunk), 1)
    causal = col <= row
    tril = causal.astype(BF16)
    mid = chunk // 2
    gn = gn_ref[...]
    log_lb = loglb_ref[...]
    log1m_lb = log1mlb_ref[...]
    onem_lb = onemlb_ref[...]
    dn_t = (((1,), (1,)), ((), ()))
    dn_0 = (((0,), (0,)), ((), ()))

    for s, u in [(s, u) for s in range(seqs) for u in range(subs)]:
        r0 = (s * subs + u) * chunk
        rs = slice(r0, r0 + chunk)
        q = q_ref[rs, :].astype(F32)
        z = z_ref[rs, :]
        g = g_ref[rs, :].astype(F32)
        vb = i_ref[rs, :]

        e = jnp.exp(-jnp.abs(z))
        r = 1.0 / (1.0 + e)
        log_sig = jnp.minimum(z, 0.0) - jnp.log(1.0 + e)
        b_ = log1m_lb + log_sig
        log_f = jnp.maximum(log_lb, b_) + jnp.log(1.0 + jnp.exp(-jnp.abs(log_lb - b_)))
        k = onem_lb * jnp.where(z >= 0.0, e * r, r)

        f1, f2, f3 = _split3(log_f)
        cum = (jnp.dot(tril, f1, preferred_element_type=F32)
               + jnp.dot(tril, f2, preferred_element_type=F32)
               + jnp.dot(tril, f3, preferred_element_type=F32))
        last = cum[chunk - 1:chunk, :]
        ref_row = cum[mid:mid + 1, :]

        dec = jnp.exp(cum - ref_row)
        q_dec = q * dec
        k_dec = k * (1.0 / dec)
        qs = q_dec.astype(BF16)
        ks = k_dec.astype(BF16)
        qe = (q_dec * jnp.exp(ref_row)).astype(BF16)
        ke = (k_dec * jnp.exp(last - ref_row)).astype(BF16)
        e_last = jnp.exp(last)

        lanes = [slice(h * A_DK, (h + 1) * A_DK) for h in range(A_HEADS)]
        scores = [lax.dot_general(qs[:, sl], ks[:, sl], dn_t, preferred_element_type=F32) for sl in lanes]
        sts = [st_ref[s * A_HEADS + h] for h in range(A_HEADS)]
        inter = [lax.dot_general(qe[:, sl], st.astype(BF16), dn_t, preferred_element_type=F32)
                 for sl, st in zip(lanes, sts)]
        kv = [lax.dot_general(vb[:, sl], ke[:, sl], dn_0, preferred_element_type=F32) for sl in lanes]
        probs = [jnp.where(causal, sc, 0.0).astype(BF16) for sc in scores]
        outs = [jnp.dot(p, vb[:, sl], preferred_element_type=F32) + it
                for p, sl, it in zip(probs, lanes, inter)]
        for h, sl in enumerate(lanes):
            st_ref[s * A_HEADS + h] = e_last[:, sl] * sts[h] + kv[h]
        o = jnp.concatenate([_rms(o_h, gn) for o_h in outs], axis=-1)
        o_ref[rs, :] = (o * (g * _sigmoid(g))).astype(o_ref.dtype)

    @pl.when(c == n_chunks - 1)
    def _():
        for s in range(seqs):
            for h in range(A_HEADS):
                sout_ref[s, h] = st_ref[s * A_HEADS + h].T


def _hgrn_scan(z, qig, lb_terms, gn, s0, s_prev, *, layer, row0, n_seq, seq_len, chunk, subs, seqs, out_rows,
               name):
    n_chunks = seq_len // (subs * chunk)
    blk = seqs * subs * chunk
    assert n_chunks == 1 or seqs == 1
    blk0 = row0 // blk
    has_init = s0 is not None
    has_prev = s_prev is not None

    def rows_at(j):
        return pl.BlockSpec((blk, D_MODEL), lambda b, c: (blk0 + b * n_chunks + c, j))

    in_specs = [rows_at(0), rows_at(0), rows_at(1), rows_at(2)]
    in_specs += [pl.BlockSpec((1, D_MODEL), lambda b, c: (0, 0))] * 3
    in_specs += [pl.BlockSpec((1, A_DV), lambda b, c: (0, 0))]
    args = [qig, z, qig, qig, *lb_terms, gn]
    if has_init:
        in_specs.append(pl.BlockSpec((None, seqs, A_HEADS, A_DK, A_DV), lambda b, c: (layer, b, 0, 0, 0)))
        args.append(s0)
    aliases = {}
    if has_prev:
        in_specs.append(pl.BlockSpec(memory_space=pl.ANY))
        aliases = {len(args): 1}
        args.append(s_prev)
    return pl.pallas_call(
        functools.partial(_hgrn_body, chunk=chunk, subs=subs, seqs=seqs, has_init=has_init,
                          has_prev_out=has_prev),
        grid=(n_seq // seqs, n_chunks),
        in_specs=in_specs,
        out_specs=[
            pl.BlockSpec((blk, D_MODEL), lambda b, c: (b * n_chunks + c, 0)),
            pl.BlockSpec((None, seqs, A_HEADS, A_DK, A_DV), lambda b, c: (layer, b, 0, 0, 0)),
        ],
        out_shape=[
            jax.ShapeDtypeStruct((out_rows, D_MODEL), BF16),
            jax.ShapeDtypeStruct((N_A_LAYERS, n_seq, A_HEADS, A_DK, A_DV), F32),
        ],
        scratch_shapes=[pltpu.VMEM((seqs * A_HEADS, A_DV, A_DK), F32)],
        input_output_aliases=aliases,
        compiler_params=_cparams("parallel", "arbitrary"),
        name=name,
    )(*args)


def _swa_prompt_body(sink_ref, q_ref, kvp_ref, kvc_ref, o_ref):
    i = pl.program_id(1)
    key = lax.broadcasted_iota(jnp.int32, (2 * WINDOW, WINDOW), 0)
    qry = lax.broadcasted_iota(jnp.int32, (2 * WINDOW, WINDOW), 1)
    mask = ((key < WINDOW) & (key >= qry) & (i > 0)) | ((key >= WINDOW) & (key - WINDOW <= qry))
    scale = B_HEAD_DIM ** -0.5
    kv_w = B_KV_HEADS * B_HEAD_DIM
    dn_t = (((1,), (1,)), ((), ()))
    dn_0 = (((0,), (0,)), ((), ()))
    outs = []
    for kh in range(B_KV_HEADS):
        ksl = slice(kh * B_HEAD_DIM, (kh + 1) * B_HEAD_DIM)
        vsl = slice(kv_w + kh * B_HEAD_DIM, kv_w + (kh + 1) * B_HEAD_DIM)
        k = jnp.concatenate([kvp_ref[:, ksl], kvc_ref[:, ksl]], axis=0).astype(BF16)
        v = jnp.concatenate([kvp_ref[:, vsl], kvc_ref[:, vsl]], axis=0).astype(BF16)
        heads = range(kh * B_GROUP, (kh + 1) * B_GROUP)
        q = jnp.concatenate([q_ref[:, h * B_HEAD_DIM:(h + 1) * B_HEAD_DIM] for h in heads], axis=0)
        q = (q * scale).astype(BF16)
        s_t = lax.dot_general(k, q, dn_t, preferred_element_type=F32)
        p_parts, inv_parts = [], []
        for gi, h in enumerate(heads):
            s = jnp.where(mask, s_t[:, gi * WINDOW:(gi + 1) * WINDOW], NEG_INF)
            sink = sink_ref[h]
            m = jnp.maximum(jnp.max(s, axis=0, keepdims=True), sink)
            p = jnp.exp(s - m)
            inv_parts.append(1.0 / (jnp.sum(p, axis=0, keepdims=True) + jnp.exp(sink - m)))
            p_parts.append(p.astype(BF16))
        p_t = jnp.concatenate(p_parts, axis=1)
        o_t = lax.dot_general(v, p_t, dn_0, preferred_element_type=F32)
        o_t = o_t * jnp.concatenate(inv_parts, axis=1)
        outs += [o_t[:, gi * WINDOW:(gi + 1) * WINDOW] for gi in range(B_GROUP)]
    o_ref[...] = jnp.concatenate(outs, axis=0).T.astype(o_ref.dtype)


def _swa_prompt(q, kv, sinks, *, n_seq, seq_len, out_rows):
    nb = seq_len // WINDOW
    kv_cols = kv.shape[1]
    return pl.pallas_call(
        _swa_prompt_body,
        grid=(n_seq, nb),
        in_specs=[
            pl.BlockSpec(memory_space=pltpu.SMEM),
            pl.BlockSpec((WINDOW, D_MODEL), lambda b, i: (b * nb + i, 0)),
            pl.BlockSpec((WINDOW, kv_cols), lambda b, i: (b * nb + jnp.maximum(i - 1, 0), 0)),
            pl.BlockSpec((WINDOW, kv_cols), lambda b, i: (b * nb + i, 0)),
        ],
        out_specs=pl.BlockSpec((WINDOW, D_MODEL), lambda b, i: (b * nb + i, 0)),
        out_shape=jax.ShapeDtypeStruct((out_rows, D_MODEL), BF16),
        compiler_params=_cparams("parallel", "arbitrary"),
        name="swa_prompt",
    )(sinks, q, kv, kv)


def _swa_sample_body(sink_ref, q_ref, kc_ref, vc_ref, kvn_ref, o_ref, *, t_new):
    rows = B_GROUP * t_new
    t_row = lax.broadcasted_iota(jnp.int32, (rows, WINDOW), 0) % t_new
    c_col = lax.broadcasted_iota(jnp.int32, (rows, WINDOW), 1)
    mask_cache = c_col >= t_row
    t_row_n = lax.broadcasted_iota(jnp.int32, (rows, t_new), 0) % t_new
    t_col_n = lax.broadcasted_iota(jnp.int32, (rows, t_new), 1)
    mask_new = t_col_n <= t_row_n
    scale = B_HEAD_DIM ** -0.5
    kv_w = B_KV_HEADS * B_HEAD_DIM
    dn = (((1,), (1,)), ((), ()))
    pairs = [(s, kh) for s in range(SAMPLE_SEQS_PER_STEP) for kh in range(B_KV_HEADS)]
    scores = []
    for s, kh in pairs:
        ksl = slice(kh * B_HEAD_DIM, (kh + 1) * B_HEAD_DIM)
        q = (q_ref[s, kh] * scale).astype(BF16)
        s_c = lax.dot_general(q, kc_ref[s, :, ksl].astype(BF16), dn, preferred_element_type=F32)
        s_n = lax.dot_general(q, kvn_ref[s, :, ksl].astype(BF16), dn, preferred_element_type=F32)
        scores.append((s_c, s_n))
    probs = []
    for (s, kh), (s_c, s_n) in zip(pairs, scores):
        s_c = jnp.where(mask_cache, s_c, NEG_INF)
        s_n = jnp.where(mask_new, s_n, NEG_INF)
        sink = sink_ref[kh]
        m = jnp.maximum(jnp.maximum(jnp.max(s_c, axis=-1, keepdims=True),
                                    jnp.max(s_n, axis=-1, keepdims=True)), sink)
        p_c = jnp.exp(s_c - m)
        p_n = jnp.exp(s_n - m)
        denom = (jnp.sum(p_c, axis=-1, keepdims=True) + jnp.sum(p_n, axis=-1, keepdims=True)
                 + jnp.exp(sink - m))
        probs.append((p_c.astype(BF16), p_n.astype(BF16), 1.0 / denom))
    for (s, kh), (p_c, p_n, inv) in zip(pairs, probs):
        ksl = slice(kh * B_HEAD_DIM, (kh + 1) * B_HEAD_DIM)
        vsl = slice(kv_w + kh * B_HEAD_DIM, kv_w + (kh + 1) * B_HEAD_DIM)
        pv = (jnp.dot(p_c, vc_ref[s, :, ksl].astype(BF16), preferred_element_type=F32)
              + jnp.dot(p_n, kvn_ref[s, :, vsl].astype(BF16), preferred_element_type=F32))
        o_ref[s, kh] = (pv * inv).astype(o_ref.dtype)


def _swa_sample(q, cache_k, cache_v, kv_new, sink_rows, *, t_new):
    n_seq = q.shape[0]
    rows = B_GROUP * t_new
    sb = SAMPLE_SEQS_PER_STEP
    kv_w = B_KV_HEADS * B_HEAD_DIM
    return pl.pallas_call(
        functools.partial(_swa_sample_body, t_new=t_new),
        grid=(n_seq // sb,),
        in_specs=[
            pl.BlockSpec((B_KV_HEADS, rows, 1), lambda b: (0, 0, 0)),
            pl.BlockSpec((sb, B_KV_HEADS, rows, B_HEAD_DIM), lambda b: (b, 0, 0, 0)),
            pl.BlockSpec((sb, WINDOW, kv_w), lambda b: (b, 0, 0)),
            pl.BlockSpec((sb, WINDOW, kv_w), lambda b: (b, 0, 0)),
            pl.BlockSpec((sb, t_new, 2 * kv_w), lambda b: (b, 0, 0)),
        ],
        out_specs=pl.BlockSpec((sb, B_KV_HEADS, rows, B_HEAD_DIM), lambda b: (b, 0, 0, 0)),
        out_shape=jax.ShapeDtypeStruct((n_seq, B_KV_HEADS, rows, B_HEAD_DIM), BF16),
        compiler_params=_cparams("parallel"),
        name="swa_sample",
    )(sink_rows, q, cache_k, cache_v, kv_new)


def kernel(x_prompt, x_sample, state_hgrn, cache_k_window, cache_v_window, norm_g, ffn_w_in, ffn_w_out,
           a_w_in, a_lower_bounds, a_gnorm, a_w_out, kv_norm, kv_w, kv_b, b_w_q, b_b_q, b_sinks, b_w_o, b_b_o):
    n_pb, p_len, _ = x_prompt.shape
    n_sb, s_len, _ = x_sample.shape
    n_p = n_pb * p_len
    n_s = n_sb * s_len
    kv_cols = B_KV_HEADS * B_HEAD_DIM

    x = jnp.concatenate([x_prompt.reshape(n_p, D_MODEL), x_sample.reshape(n_s, D_MODEL)], axis=0)

    ffn_w_in_b = ffn_w_in.astype(BF16)
    ffn_w_out_b = ffn_w_out.astype(BF16)
    a_w_in_b = a_w_in.astype(BF16)
    a_w_out_b = a_w_out.astype(BF16)
    kv_w_b = kv_w.astype(BF16)
    b_w_q_b = b_w_q.astype(BF16)
    b_w_o_b = b_w_o.astype(BF16)
    zero_bias = jnp.zeros((1, D_MODEL), F32)

    lb_soft = jax.nn.softmax(a_lower_bounds.astype(F32), axis=0)
    lbs = jnp.cumsum(lb_soft, axis=0) - lb_soft[0]
    pos = lbs > 0
    log_lb = jnp.where(pos, jnp.log(jnp.where(pos, lbs, 1.0)), NEG_INF)
    log1m_lb = jnp.log1p(-lbs)
    onem_lb = 1.0 - lbs

    cache_k = cache_k_window.reshape(n_sb, WINDOW, kv_cols)
    cache_v = cache_v_window.reshape(n_sb, WINDOW, kv_cols)

    def g_row(l, j):
        return norm_g[l, j].reshape(1, D_MODEL)

    s_prompt = s_sample = None
    kv = None
    for l in range(DEPTH):
        x = _ffn(x, g_row(l, 0), g_row(l, 1), ffn_w_in_b, ffn_w_out_b, l, 0, unroll=(3, 9, 1, 1)[l])
        if l < N_A_LAYERS:
            z, qig = _hgrn_in(x, g_row(l, 2), a_w_in_b, l)
            lb_terms = (log_lb[l].reshape(1, D_MODEL), log1m_lb[l].reshape(1, D_MODEL),
                        onem_lb[l].reshape(1, D_MODEL))
            gn = a_gnorm[l].reshape(1, A_DV)
            p_chunk = math.gcd(p_len, PROMPT_CHUNK)
            o_p, s_prompt = _hgrn_scan(z, qig, lb_terms, gn, None, s_prompt, layer=l, row0=0, n_seq=n_pb,
                                       seq_len=p_len, chunk=p_chunk,
                                       subs=math.gcd(p_len // p_chunk, PROMPT_CHUNKS_PER_STEP), seqs=1,
                                       out_rows=n_p + n_s, name="hgrn_prompt")
            o_s, s_sample = _hgrn_scan(z, qig, lb_terms, gn, state_hgrn, s_sample, layer=l, row0=n_p,
                                       n_seq=n_sb, seq_len=s_len, chunk=s_len, subs=1,
                                       seqs=SAMPLE_SEQS_PER_SCAN_STEP, out_rows=n_s, name="hgrn_sample")
            o = lax.dynamic_update_slice(o_p, o_s, (n_p, 0))
            x = _post_mm(o, a_w_out_b[l], zero_bias, g_row(l, 3), x, "hgrn_out")
        else:
            j = l - N_A_LAYERS
            q = _pre_mm(x, g_row(l, 2), b_w_q_b[j], b_b_q[j].reshape(1, D_MODEL), "swa_q")
            o_p = _swa_prompt(q, kv, b_sinks[j], n_seq=n_pb, seq_len=p_len, out_rows=n_p + n_s)
            q_s = q[n_p:].reshape(n_sb, s_len, B_KV_HEADS, B_GROUP, B_HEAD_DIM)
            q_s = q_s.transpose(0, 2, 3, 1, 4).reshape(n_sb, B_KV_HEADS, B_GROUP * s_len, B_HEAD_DIM)
            sink_rows = jnp.repeat(b_sinks[j].reshape(B_KV_HEADS, B_GROUP), s_len, axis=1)[..., None]
            o_s = _swa_sample(q_s, cache_k, cache_v, kv[n_p:].reshape(n_sb, s_len, 2 * kv_cols),
                              sink_rows, t_new=s_len)
            o_s = o_s.reshape(n_sb, B_KV_HEADS, B_GROUP, s_len, B_HEAD_DIM)
            o_s = o_s.transpose(0, 3, 1, 2, 4).reshape(n_s, D_MODEL)
            o = lax.dynamic_update_slice(o_p, o_s, (n_p, 0))
            x = _post_mm(o, b_w_o_b[j], b_b_o[j].reshape(1, D_MODEL), g_row(l, 3), x, "swa_out")
        x = _ffn(x, g_row(l, 4), g_row(l, 5), ffn_w_in_b, ffn_w_out_b, l, 1)
        if l == N_A_LAYERS - 1:
            kv = _pre_mm(x, kv_norm.reshape(1, D_MODEL), kv_w_b, kv_b.reshape(1, 2 * kv_cols), "shared_kv")

    y_prompt = x[:n_p].reshape(n_pb, p_len, D_MODEL)
    y_sample = x[n_p:].reshape(n_sb, s_len, D_MODEL)

    kv_p = jnp.stack([kv[(b + 1) * p_len - WINDOW:(b + 1) * p_len] for b in range(n_pb)])
    kv_p = kv_p.reshape(n_pb, WINDOW, 2, B_KV_HEADS, B_HEAD_DIM)
    kv_s = kv[n_p:].reshape(n_sb, s_len, 2, B_KV_HEADS, B_HEAD_DIM)
    k_win_sample = jnp.concatenate([cache_k_window, kv_s[:, :, 0]], axis=1)[:, -WINDOW:]
    v_win_sample = jnp.concatenate([cache_v_window, kv_s[:, :, 1]], axis=1)[:, -WINDOW:]
    return (y_prompt, y_sample, s_prompt, s_sample,
            kv_p[:, :, 0], kv_p[:, :, 1], k_win_sample, v_win_sample)
```

```python
import functools
import math

import jax
import jax.numpy as jnp
from jax import lax
from jax.experimental import pallas as pl
from jax.experimental.pallas import tpu as pltpu

F32 = jnp.float32
BF16 = jnp.bfloat16

D_MODEL = 1024
DEPTH = 4
N_A_LAYERS = 2
A_HEADS = 8
A_DK = 128
A_DV = 128
B_HEAD_DIM = 64
B_HEADS = 16
B_KV_HEADS = 2
B_GROUP = 8
WINDOW = 128
D_FF = 2816
EPS = 1e-6
NEG_INF = -1e30

VMEM_LIMIT_BYTES = 56 * 1024 * 1024

ROW_TILE = 1024
FFN_TILE = 256
FFN_ROW_SPLIT = 2
PROMPT_CHUNK = 64
PROMPT_CHUNKS_PER_STEP = 8
SAMPLE_SEQS_PER_STEP = 8
SAMPLE_SEQS_PER_SCAN_STEP = 4


def _rms(x, g):
    return x * lax.rsqrt(jnp.mean(x * x, axis=-1, keepdims=True) + EPS) * g


def _sigmoid(x):
    return 1.0 / (1.0 + jnp.exp(-x))


def _cparams(*sem):
    return pltpu.CompilerParams(dimension_semantics=sem, vmem_limit_bytes=VMEM_LIMIT_BYTES)


def _lead_tail_specs(n_lead_tiles, cols):
    return [pl.BlockSpec((ROW_TILE, cols), lambda i: (jnp.minimum(i, n_lead_tiles - 1), 0)),
            pl.BlockSpec((ROW_TILE, cols), lambda i: (jnp.maximum(i - n_lead_tiles, 0), 0))]


def _ffn_body(*refs, n_src, n_dst, n_lead_tiles):
    refs = list(refs)
    x_refs = [refs.pop(0) for _ in range(n_src)]
    gpre_ref, gpost_ref, win_ref, wout_ref = [refs.pop(0) for _ in range(4)]
    out_refs = [refs.pop(0) for _ in range(n_dst)]
    h_ref, acc_ref = refs.pop(0), refs.pop(0)
    i = pl.program_id(0)
    nf = D_FF // FFN_TILE
    rows = ROW_TILE // FFN_ROW_SPLIT

    if n_src == 2:
        x_ref = refs.pop(0)

        @pl.when(i < n_lead_tiles)
        def _():
            x_ref[...] = x_refs[0][...]

        @pl.when(i >= n_lead_tiles)
        def _():
            x_ref[...] = x_refs[1][...]
    else:
        x_ref = x_refs[0]
    out_ref = out_refs[-1]

    for f in range(nf):
        wg = win_ref[:, f * FFN_TILE:(f + 1) * FFN_TILE]
        wu = win_ref[:, D_FF + f * FFN_TILE:D_FF + (f + 1) * FFN_TILE]
        wo = wout_ref[f * FFN_TILE:(f + 1) * FFN_TILE, :]
        for r in range(FFN_ROW_SPLIT):
            rs = slice(r * rows, (r + 1) * rows)
            if f == 0:
                h = _rms(x_ref[rs, :], gpre_ref[...]).astype(BF16)
                h_ref[rs, :] = h
            else:
                h = h_ref[rs, :]
            gate = jnp.dot(h, wg, preferred_element_type=F32)
            up = jnp.dot(h, wu, preferred_element_type=F32)
            act = (gate * _sigmoid(gate) * up).astype(BF16)
            part = jnp.dot(act, wo, preferred_element_type=F32)
            if f == 0:
                acc_ref[rs, :] = part
            elif f == nf - 1:
                out_ref[rs, :] = x_ref[rs, :] + 0.5 * _rms(acc_ref[rs, :] + part, gpost_ref[...])
            else:
                acc_ref[rs, :] += part

    if n_dst == 2:
        @pl.when(i < n_lead_tiles)
        def _():
            out_refs[0][...] = out_ref[...]


def _ffn(xs, g_pre, g_post, w_in, w_out, layer, which, split_rows=None):
    xs = xs if isinstance(xs, (tuple, list)) else (xs,)
    n = sum(x.shape[0] for x in xs)
    assert D_FF // FFN_TILE >= 2
    n_lead = xs[0].shape[0] if len(xs) == 2 else (split_rows[0] if split_rows else n)
    assert split_rows is None or len(xs) == 1 or split_rows[0] == xs[0].shape[0]
    n_lead_tiles = n_lead // ROW_TILE
    resident = pl.Buffered(1)
    row_spec = pl.BlockSpec((ROW_TILE, D_MODEL), lambda i: (i, 0))
    x_specs = _lead_tail_specs(n_lead_tiles, D_MODEL) if len(xs) == 2 else [row_spec]
    scratch = [pltpu.VMEM((ROW_TILE, D_MODEL), BF16), pltpu.VMEM((ROW_TILE, D_MODEL), F32)]
    if len(xs) == 2:
        scratch.append(pltpu.VMEM((ROW_TILE, D_MODEL), F32))
    if split_rows:
        out_specs = _lead_tail_specs(n_lead_tiles, D_MODEL)
        out_shape = [jax.ShapeDtypeStruct((r, D_MODEL), F32) for r in split_rows]
    else:
        out_specs = row_spec
        out_shape = jax.ShapeDtypeStruct((n, D_MODEL), F32)
    return pl.pallas_call(
        functools.partial(_ffn_body, n_src=len(xs), n_dst=2 if split_rows else 1, n_lead_tiles=n_lead_tiles),
        grid=(n // ROW_TILE,),
        in_specs=x_specs + [
            pl.BlockSpec((1, D_MODEL), lambda i: (0, 0)),
            pl.BlockSpec((1, D_MODEL), lambda i: (0, 0)),
            pl.BlockSpec((None, None, D_MODEL, 2 * D_FF), lambda i: (layer, which, 0, 0), pipeline_mode=resident),
            pl.BlockSpec((None, None, D_FF, D_MODEL), lambda i: (layer, which, 0, 0), pipeline_mode=resident),
        ],
        out_specs=out_specs,
        out_shape=out_shape,
        scratch_shapes=scratch,
        compiler_params=_cparams("arbitrary"),
        name="ffn",
    )(*xs, g_pre, g_post, w_in, w_out)


def _pre_mm_body(x_ref, g_ref, w_ref, b_ref, out_ref, h_ref):
    @pl.when(pl.program_id(1) == 0)
    def _():
        h_ref[...] = _rms(x_ref[...], g_ref[...]).astype(BF16)

    out_ref[...] = jnp.dot(h_ref[...], w_ref[...], preferred_element_type=F32) + b_ref[...]


def _pre_mm(x, g, w, b, name):
    n = x.shape[0]
    n_out = w.shape[1]
    tn = min(n_out, 1024)
    return pl.pallas_call(
        _pre_mm_body,
        grid=(n // ROW_TILE, n_out // tn),
        in_specs=[
            pl.BlockSpec((ROW_TILE, D_MODEL), lambda i, j: (i, 0)),
            pl.BlockSpec((1, D_MODEL), lambda i, j: (0, 0)),
            pl.BlockSpec((D_MODEL, tn), lambda i, j: (0, j)),
            pl.BlockSpec((1, tn), lambda i, j: (0, j)),
        ],
        out_specs=pl.BlockSpec((ROW_TILE, tn), lambda i, j: (i, j)),
        out_shape=jax.ShapeDtypeStruct((n, n_out), F32),
        scratch_shapes=[pltpu.VMEM((ROW_TILE, D_MODEL), BF16)],
        compiler_params=_cparams("parallel", "arbitrary"),
        name=name,
    )(x, g, w, b)


def _hgrn_in_body(x_ref, g_ref, w_ref, z_ref, qig_ref):
    rows = ROW_TILE // 2
    d = D_MODEL
    for r in range(2):
        rs = slice(r * rows, (r + 1) * rows)
        h = _rms(x_ref[rs, :], g_ref[...]).astype(BF16)
        z_ref[rs, :] = jnp.dot(h, w_ref[:, d:2 * d], preferred_element_type=F32)
        for j, c0 in enumerate((0, 2 * d, 3 * d)):
            qig_ref[rs, j * d:(j + 1) * d] = jnp.dot(h, w_ref[:, c0:c0 + d],
                                                     preferred_element_type=F32).astype(BF16)


def _hgrn_in(x, g, w, layer):
    n = x.shape[0]
    return pl.pallas_call(
        _hgrn_in_body,
        grid=(n // ROW_TILE,),
        in_specs=[
            pl.BlockSpec((ROW_TILE, D_MODEL), lambda i: (i, 0)),
            pl.BlockSpec((1, D_MODEL), lambda i: (0, 0)),
            pl.BlockSpec((None, D_MODEL, 4 * D_MODEL), lambda i: (layer, 0, 0), pipeline_mode=pl.Buffered(1)),
        ],
        out_specs=[
            pl.BlockSpec((ROW_TILE, D_MODEL), lambda i: (i, 0)),
            pl.BlockSpec((ROW_TILE, 3 * D_MODEL), lambda i: (i, 0)),
        ],
        out_shape=[
            jax.ShapeDtypeStruct((n, D_MODEL), F32),
            jax.ShapeDtypeStruct((n, 3 * D_MODEL), BF16),
        ],
        compiler_params=_cparams("parallel"),
        name="hgrn_in",
    )(x, g, w)


def _post_mm_body(o_lead_ref, o_tail_ref, w_ref, b_ref, g_ref, x_ref, out_ref, *, n_lead_tiles):
    i = pl.program_id(0)

    def run(o_ref):
        m = jnp.dot(o_ref[...], w_ref[...], preferred_element_type=F32) + b_ref[...]
        out_ref[...] = x_ref[...] + _rms(m, g_ref[...])

    pl.when(i < n_lead_tiles)(functools.partial(run, o_lead_ref))
    pl.when(i >= n_lead_tiles)(functools.partial(run, o_tail_ref))


def _post_mm(o_lead, o_tail, w, b, g, x, name):
    n = x.shape[0]
    n_lead_tiles = o_lead.shape[0] // ROW_TILE
    return pl.pallas_call(
        functools.partial(_post_mm_body, n_lead_tiles=n_lead_tiles),
        grid=(n // ROW_TILE,),
        in_specs=_lead_tail_specs(n_lead_tiles, D_MODEL) + [
            pl.BlockSpec((D_MODEL, D_MODEL), lambda i: (0, 0)),
            pl.BlockSpec((1, D_MODEL), lambda i: (0, 0)),
            pl.BlockSpec((1, D_MODEL), lambda i: (0, 0)),
            pl.BlockSpec((ROW_TILE, D_MODEL), lambda i: (i, 0)),
        ],
        out_specs=pl.BlockSpec((ROW_TILE, D_MODEL), lambda i: (i, 0)),
        out_shape=jax.ShapeDtypeStruct((n, D_MODEL), F32),
        compiler_params=_cparams("parallel"),
        name=name,
    )(o_lead, o_tail, w, b, g, x)


def _split3(a):
    a1 = a.astype(BF16)
    r1 = a - a1.astype(F32)
    a2 = r1.astype(BF16)
    a3 = (r1 - a2.astype(F32)).astype(BF16)
    return a1, a2, a3


def _hgrn_body(*refs, chunk, subs, seqs, layer, has_init, has_prev_out):
    refs = list(refs)
    q_ref, z_ref, i_ref, g_ref, loglb_ref, log1mlb_ref, onemlb_ref, gn_ref = refs[:8]
    del refs[:8]
    s0_ref = refs.pop(0) if has_init else None
    if has_prev_out:
        refs.pop(0)
    o_ref, sout_ref, st_ref = refs
    c = pl.program_id(1)
    n_chunks = pl.num_programs(1)

    @pl.when(c == 0)
    def _():
        if has_init:
            for s in range(seqs):
                for h in range(A_HEADS):
                    st_ref[s * A_HEADS + h] = s0_ref[s, h].T
        else:
            st_ref[...] = jnp.zeros_like(st_ref)

    row = lax.broadcasted_iota(jnp.int32, (chunk, chunk), 0)
    col = lax.broadcasted_iota(jnp.int32, (chunk, chunk), 1)
    causal = col <= row
    tril = causal.astype(BF16)
    mid = chunk // 2
    gn = gn_ref[...]
    log_lb = loglb_ref[...]
    log1m_lb = log1mlb_ref[...]
    onem_lb = onemlb_ref[...]
    dn_t = (((1,), (1,)), ((), ()))
    dn_0 = (((0,), (0,)), ((), ()))

    for s, u in [(s, u) for s in range(seqs) for u in range(subs)]:
        r0 = (s * subs + u) * chunk
        rs = slice(r0, r0 + chunk)
        q = q_ref[rs, :].astype(F32)
        z = z_ref[rs, :]
        g = g_ref[rs, :].astype(F32)
        vb = i_ref[rs, :]

        e = jnp.exp(-jnp.abs(z))
        r = 1.0 / (1.0 + e)
        log_sig = jnp.minimum(z, 0.0) - jnp.log(1.0 + e)
        b_ = log1m_lb + log_sig
        log_f = jnp.maximum(log_lb, b_) + jnp.log(1.0 + jnp.exp(-jnp.abs(log_lb - b_)))
        k = onem_lb * jnp.where(z >= 0.0, e * r, r)

        f1, f2, f3 = _split3(log_f)
        cum = (jnp.dot(tril, f1, preferred_element_type=F32)
               + jnp.dot(tril, f2, preferred_element_type=F32)
               + jnp.dot(tril, f3, preferred_element_type=F32))
        last = cum[chunk - 1:chunk, :]
        ref_row = cum[mid:mid + 1, :]

        dec = jnp.exp(cum - ref_row)
        q_dec = q * dec
        k_dec = k * (1.0 / dec)
        qs = q_dec.astype(BF16)
        ks = k_dec.astype(BF16)
        qe = (q_dec * jnp.exp(ref_row)).astype(BF16)
        ke = (k_dec * jnp.exp(last - ref_row)).astype(BF16)
        e_last = jnp.exp(last)

        lanes = [slice(h * A_DK, (h + 1) * A_DK) for h in range(A_HEADS)]
        scores = [lax.dot_general(qs[:, sl], ks[:, sl], dn_t, preferred_element_type=F32) for sl in lanes]
        sts = [st_ref[s * A_HEADS + h] for h in range(A_HEADS)]
        inter = [lax.dot_general(qe[:, sl], st.astype(BF16), dn_t, preferred_element_type=F32)
                 for sl, st in zip(lanes, sts)]
        kv = [lax.dot_general(vb[:, sl], ke[:, sl], dn_0, preferred_element_type=F32) for sl in lanes]
        probs = [jnp.where(causal, sc, 0.0).astype(BF16) for sc in scores]
        outs = [jnp.dot(p, vb[:, sl], preferred_element_type=F32) + it
                for p, sl, it in zip(probs, lanes, inter)]
        for h, sl in enumerate(lanes):
            st_ref[s * A_HEADS + h] = e_last[:, sl] * sts[h] + kv[h]
        o = jnp.concatenate([_rms(o_h, gn) for o_h in outs], axis=-1)
        o_ref[rs, :] = (o * (g * _sigmoid(g))).astype(o_ref.dtype)

    @pl.when(c == n_chunks - 1)
    def _():
        own = sout_ref if has_prev_out else sout_ref.at[layer]
        for s in range(seqs):
            for h in range(A_HEADS):
                own[s, h] = st_ref[s * A_HEADS + h].T
        if not has_prev_out:
            for other in range(N_A_LAYERS):
                if other != layer:
                    sout_ref[other] = jnp.zeros(sout_ref.shape[1:], sout_ref.dtype)


def _hgrn_scan(z, qig, lb_terms, gn, s0, s_prev, *, layer, row0, n_seq, seq_len, chunk, subs, seqs, name):
    n_chunks = seq_len // (subs * chunk)
    blk = seqs * subs * chunk
    assert n_chunks == 1 or seqs == 1
    blk0 = row0 // blk
    has_init = s0 is not None
    has_prev = s_prev is not None

    def rows_at(j):
        return pl.BlockSpec((blk, D_MODEL), lambda b, c: (blk0 + b * n_chunks + c, j))

    in_specs = [rows_at(0), rows_at(0), rows_at(1), rows_at(2)]
    in_specs += [pl.BlockSpec((1, D_MODEL), lambda b, c: (0, 0))] * 3
    in_specs += [pl.BlockSpec((1, A_DV), lambda b, c: (0, 0))]
    args = [qig, z, qig, qig, *lb_terms, gn]
    if has_init:
        in_specs.append(pl.BlockSpec((None, seqs, A_HEADS, A_DK, A_DV), lambda b, c: (layer, b, 0, 0, 0)))
        args.append(s0)
    aliases = {}
    if has_prev:
        in_specs.append(pl.BlockSpec(memory_space=pl.ANY))
        aliases = {len(args): 1}
        args.append(s_prev)
    if has_prev:
        state_spec = pl.BlockSpec((None, seqs, A_HEADS, A_DK, A_DV), lambda b, c: (layer, b, 0, 0, 0))
    else:
        state_spec = pl.BlockSpec((N_A_LAYERS, seqs, A_HEADS, A_DK, A_DV), lambda b, c: (0, b, 0, 0, 0))
    return pl.pallas_call(
        functools.partial(_hgrn_body, chunk=chunk, subs=subs, seqs=seqs, layer=layer, has_init=has_init,
                          has_prev_out=has_prev),
        grid=(n_seq // seqs, n_chunks),
        in_specs=in_specs,
        out_specs=[pl.BlockSpec((blk, D_MODEL), lambda b, c: (b * n_chunks + c, 0)), state_spec],
        out_shape=[
            jax.ShapeDtypeStruct((n_seq * seq_len, D_MODEL), BF16),
            jax.ShapeDtypeStruct((N_A_LAYERS, n_seq, A_HEADS, A_DK, A_DV), F32),
        ],
        scratch_shapes=[pltpu.VMEM((seqs * A_HEADS, A_DV, A_DK), F32)],
        input_output_aliases=aliases,
        compiler_params=_cparams("parallel", "arbitrary"),
        name=name,
    )(*args)


def _swa_prompt_body(sink_ref, q_ref, kvp_ref, kvc_ref, o_ref):
    i = pl.program_id(1)
    key = lax.broadcasted_iota(jnp.int32, (2 * WINDOW, WINDOW), 0)
    qry = lax.broadcasted_iota(jnp.int32, (2 * WINDOW, WINDOW), 1)
    mask = ((key < WINDOW) & (key >= qry) & (i > 0)) | ((key >= WINDOW) & (key - WINDOW <= qry))
    scale = B_HEAD_DIM ** -0.5
    kv_w = B_KV_HEADS * B_HEAD_DIM
    dn_t = (((1,), (1,)), ((), ()))
    dn_0 = (((0,), (0,)), ((), ()))
    outs = []
    for kh in range(B_KV_HEADS):
        ksl = slice(kh * B_HEAD_DIM, (kh + 1) * B_HEAD_DIM)
        vsl = slice(kv_w + kh * B_HEAD_DIM, kv_w + (kh + 1) * B_HEAD_DIM)
        k = jnp.concatenate([kvp_ref[:, ksl], kvc_ref[:, ksl]], axis=0).astype(BF16)
        v = jnp.concatenate([kvp_ref[:, vsl], kvc_ref[:, vsl]], axis=0).astype(BF16)
        heads = range(kh * B_GROUP, (kh + 1) * B_GROUP)
        q = jnp.concatenate([q_ref[:, h * B_HEAD_DIM:(h + 1) * B_HEAD_DIM] for h in heads], axis=0)
        q = (q * scale).astype(BF16)
        s_t = lax.dot_general(k, q, dn_t, preferred_element_type=F32)
        p_parts, inv_parts = [], []
        for gi, h in enumerate(heads):
            s = jnp.where(mask, s_t[:, gi * WINDOW:(gi + 1) * WINDOW], NEG_INF)
            sink = sink_ref[h]
            m = jnp.maximum(jnp.max(s, axis=0, keepdims=True), sink)
            p = jnp.exp(s - m)
            inv_parts.append(1.0 / (jnp.sum(p, axis=0, keepdims=True) + jnp.exp(sink - m)))
            p_parts.append(p.astype(BF16))
        p_t = jnp.concatenate(p_parts, axis=1)
        o_t = lax.dot_general(v, p_t, dn_0, preferred_element_type=F32)
        o_t = o_t * jnp.concatenate(inv_parts, axis=1)
        outs += [o_t[:, gi * WINDOW:(gi + 1) * WINDOW] for gi in range(B_GROUP)]
    o_ref[...] = jnp.concatenate(outs, axis=0).T.astype(o_ref.dtype)


def _swa_prompt(q, kv, sinks, *, n_seq, seq_len):
    nb = seq_len // WINDOW
    kv_cols = kv.shape[1]
    return pl.pallas_call(
        _swa_prompt_body,
        grid=(n_seq, nb),
        in_specs=[
            pl.BlockSpec(memory_space=pltpu.SMEM),
            pl.BlockSpec((WINDOW, D_MODEL), lambda b, i: (b * nb + i, 0)),
            pl.BlockSpec((WINDOW, kv_cols), lambda b, i: (b * nb + jnp.maximum(i - 1, 0), 0)),
            pl.BlockSpec((WINDOW, kv_cols), lambda b, i: (b * nb + i, 0)),
        ],
        out_specs=pl.BlockSpec((WINDOW, D_MODEL), lambda b, i: (b * nb + i, 0)),
        out_shape=jax.ShapeDtypeStruct((n_seq * seq_len, D_MODEL), BF16),
        compiler_params=_cparams("parallel", "arbitrary"),
        name="swa_prompt",
    )(sinks, q, kv, kv)


def _swa_sample_body(sink_ref, q_ref, kc_ref, vc_ref, kvn_ref, o_ref, *, t_new):
    rows = B_GROUP * t_new
    t_row = lax.broadcasted_iota(jnp.int32, (rows, WINDOW), 0) % t_new
    c_col = lax.broadcasted_iota(jnp.int32, (rows, WINDOW), 1)
    mask_cache = c_col >= t_row
    t_row_n = lax.broadcasted_iota(jnp.int32, (rows, t_new), 0) % t_new
    t_col_n = lax.broadcasted_iota(jnp.int32, (rows, t_new), 1)
    mask_new = t_col_n <= t_row_n
    scale = B_HEAD_DIM ** -0.5
    kv_w = B_KV_HEADS * B_HEAD_DIM
    dn = (((1,), (1,)), ((), ()))
    pairs = [(s, kh) for s in range(SAMPLE_SEQS_PER_STEP) for kh in range(B_KV_HEADS)]
    scores = []
    for s, kh in pairs:
        ksl = slice(kh * B_HEAD_DIM, (kh + 1) * B_HEAD_DIM)
        q = (q_ref[s, kh] * scale).astype(BF16)
        s_c = lax.dot_general(q, kc_ref[s, :, ksl].astype(BF16), dn, preferred_element_type=F32)
        s_n = lax.dot_general(q, kvn_ref[s, :, ksl].astype(BF16), dn, preferred_element_type=F32)
        scores.append((s_c, s_n))
    probs = []
    for (s, kh), (s_c, s_n) in zip(pairs, scores):
        s_c = jnp.where(mask_cache, s_c, NEG_INF)
        s_n = jnp.where(mask_new, s_n, NEG_INF)
        sink = sink_ref[kh]
        m = jnp.maximum(jnp.maximum(jnp.max(s_c, axis=-1, keepdims=True),
                                    jnp.max(s_n, axis=-1, keepdims=True)), sink)
        p_c = jnp.exp(s_c - m)
        p_n = jnp.exp(s_n - m)
        denom = (jnp.sum(p_c, axis=-1, keepdims=True) + jnp.sum(p_n, axis=-1, keepdims=True)
                 + jnp.exp(sink - m))
        probs.append((p_c.astype(BF16), p_n.astype(BF16), 1.0 / denom))
    for (s, kh), (p_c, p_n, inv) in zip(pairs, probs):
        ksl = slice(kh * B_HEAD_DIM, (kh + 1) * B_HEAD_DIM)
        vsl = slice(kv_w + kh * B_HEAD_DIM, kv_w + (kh + 1) * B_HEAD_DIM)
        pv = (jnp.dot(p_c, vc_ref[s, :, ksl].astype(BF16), preferred_element_type=F32)
              + jnp.dot(p_n, kvn_ref[s, :, vsl].astype(BF16), preferred_element_type=F32))
        o_ref[s, kh] = (pv * inv).astype(o_ref.dtype)


def _swa_sample(q, cache_k, cache_v, kv_new, sink_rows, *, t_new):
    n_seq = q.shape[0]
    rows = B_GROUP * t_new
    sb = SAMPLE_SEQS_PER_STEP
    kv_w = B_KV_HEADS * B_HEAD_DIM
    return pl.pallas_call(
        functools.partial(_swa_sample_body, t_new=t_new),
        grid=(n_seq // sb,),
        in_specs=[
            pl.BlockSpec((B_KV_HEADS, rows, 1), lambda b: (0, 0, 0)),
            pl.BlockSpec((sb, B_KV_HEADS, rows, B_HEAD_DIM), lambda b: (b, 0, 0, 0)),
            pl.BlockSpec((sb, WINDOW, kv_w), lambda b: (b, 0, 0)),
            pl.BlockSpec((sb, WINDOW, kv_w), lambda b: (b, 0, 0)),
            pl.BlockSpec((sb, t_new, 2 * kv_w), lambda b: (b, 0, 0)),
        ],
        out_specs=pl.BlockSpec((sb, B_KV_HEADS, rows, B_HEAD_DIM), lambda b: (b, 0, 0, 0)),
        out_shape=jax.ShapeDtypeStruct((n_seq, B_KV_HEADS, rows, B_HEAD_DIM), BF16),
        compiler_params=_cparams("parallel"),
        name="swa_sample",
    )(sink_rows, q, cache_k, cache_v, kv_new)


def kernel(x_prompt, x_sample, state_hgrn, cache_k_window, cache_v_window, norm_g, ffn_w_in, ffn_w_out,
           a_w_in, a_lower_bounds, a_gnorm, a_w_out, kv_norm, kv_w, kv_b, b_w_q, b_b_q, b_sinks, b_w_o, b_b_o):
    n_pb, p_len, _ = x_prompt.shape
    n_sb, s_len, _ = x_sample.shape
    n_p = n_pb * p_len
    n_s = n_sb * s_len
    kv_cols = B_KV_HEADS * B_HEAD_DIM

    x = (x_prompt.reshape(n_p, D_MODEL), x_sample.reshape(n_s, D_MODEL))

    ffn_w_in_b = ffn_w_in.astype(BF16)
    ffn_w_out_b = ffn_w_out.astype(BF16)
    a_w_in_b = a_w_in.astype(BF16)
    a_w_out_b = a_w_out.astype(BF16)
    kv_w_b = kv_w.astype(BF16)
    b_w_q_b = b_w_q.astype(BF16)
    b_w_o_b = b_w_o.astype(BF16)
    zero_bias = jnp.zeros((1, D_MODEL), F32)

    lb_soft = jax.nn.softmax(a_lower_bounds.astype(F32), axis=0)
    lbs = jnp.cumsum(lb_soft, axis=0) - lb_soft[0]
    pos = lbs > 0
    log_lb = jnp.where(pos, jnp.log(jnp.where(pos, lbs, 1.0)), NEG_INF)
    log1m_lb = jnp.log1p(-lbs)
    onem_lb = 1.0 - lbs

    cache_k = cache_k_window.reshape(n_sb, WINDOW, kv_cols)
    cache_v = cache_v_window.reshape(n_sb, WINDOW, kv_cols)

    def g_row(l, j):
        return norm_g[l, j].reshape(1, D_MODEL)

    s_prompt = s_sample = None
    kv = None
    for l in range(DEPTH):
        x = _ffn(x, g_row(l, 0), g_row(l, 1), ffn_w_in_b, ffn_w_out_b, l, 0)
        if l < N_A_LAYERS:
            z, qig = _hgrn_in(x, g_row(l, 2), a_w_in_b, l)
            lb_terms = (log_lb[l].reshape(1, D_MODEL), log1m_lb[l].reshape(1, D_MODEL),
                        onem_lb[l].reshape(1, D_MODEL))
            gn = a_gnorm[l].reshape(1, A_DV)
            p_chunk = math.gcd(p_len, PROMPT_CHUNK)
            o_p, s_prompt = _hgrn_scan(z, qig, lb_terms, gn, None, s_prompt, layer=l, row0=0, n_seq=n_pb,
                                       seq_len=p_len, chunk=p_chunk,
                                       subs=math.gcd(p_len // p_chunk, PROMPT_CHUNKS_PER_STEP), seqs=1,
                                       name="hgrn_prompt")
            o_s, s_sample = _hgrn_scan(z, qig, lb_terms, gn, state_hgrn, s_sample, layer=l, row0=n_p,
                                       n_seq=n_sb, seq_len=s_len, chunk=s_len, subs=1,
                                       seqs=SAMPLE_SEQS_PER_SCAN_STEP, name="hgrn_sample")
            x = _post_mm(o_p, o_s, a_w_out_b[l], zero_bias, g_row(l, 3), x, "hgrn_out")
        else:
            j = l - N_A_LAYERS
            q = _pre_mm(x, g_row(l, 2), b_w_q_b[j], b_b_q[j].reshape(1, D_MODEL), "swa_q")
            o_p = _swa_prompt(q, kv, b_sinks[j], n_seq=n_pb, seq_len=p_len)
            q_s = q[n_p:].reshape(n_sb, s_len, B_KV_HEADS, B_GROUP, B_HEAD_DIM)
            q_s = q_s.transpose(0, 2, 3, 1, 4).reshape(n_sb, B_KV_HEADS, B_GROUP * s_len, B_HEAD_DIM)
            sink_rows = jnp.repeat(b_sinks[j].reshape(B_KV_HEADS, B_GROUP), s_len, axis=1)[..., None]
            o_s = _swa_sample(q_s, cache_k, cache_v, kv[n_p:].reshape(n_sb, s_len, 2 * kv_cols),
                              sink_rows, t_new=s_len)
            o_s = o_s.reshape(n_sb, B_KV_HEADS, B_GROUP, s_len, B_HEAD_DIM)
            o_s = o_s.transpose(0, 3, 1, 2, 4).reshape(n_s, D_MODEL)
            x = _post_mm(o_p, o_s, b_w_o_b[j], b_b_o[j].reshape(1, D_MODEL), g_row(l, 3), x, "swa_out")
        x = _ffn(x, g_row(l, 4), g_row(l, 5), ffn_w_in_b, ffn_w_out_b, l, 1,
                 split_rows=(n_p, n_s) if l == DEPTH - 1 else None)
        if l == N_A_LAYERS - 1:
            kv = _pre_mm(x, kv_norm.reshape(1, D_MODEL), kv_w_b, kv_b.reshape(1, 2 * kv_cols), "shared_kv")

    y_prompt = x[0].reshape(n_pb, p_len, D_MODEL)
    y_sample = x[1].reshape(n_sb, s_len, D_MODEL)

    kv_p = jnp.stack([kv[(b + 1) * p_len - WINDOW:(b + 1) * p_len] for b in range(n_pb)])
    kv_p = kv_p.reshape(n_pb, WINDOW, 2, B_KV_HEADS, B_HEAD_DIM)
    kv_s = kv[n_p:].reshape(n_sb, s_len, 2, B_KV_HEADS, B_HEAD_DIM)
    k_win_sample = jnp.concatenate([cache_k_window, kv_s[:, :, 0]], axis=1)[:, -WINDOW:]
    v_win_sample = jnp.concatenate([cache_v_window, kv_s[:, :, 1]], axis=1)[:, -WINDOW:]
    return (y_prompt, y_sample, s_prompt, s_sample,
            kv_p[:, :, 0], kv_p[:, :, 1], k_win_sample, v_win_sample)
```

```python
import functools
import math

import jax
import jax.numpy as jnp
from jax import lax
from jax.experimental import pallas as pl
from jax.experimental.pallas import tpu as pltpu

F32 = jnp.float32
BF16 = jnp.bfloat16

D_MODEL = 1024
DEPTH = 4
N_A_LAYERS = 2
A_HEADS = 8
A_DK = 128
A_DV = 128
B_HEAD_DIM = 64
B_HEADS = 16
B_KV_HEADS = 2
B_GROUP = 8
WINDOW = 128
D_FF = 2816
EPS = 1e-6
NEG_INF = -1e30

VMEM_LIMIT_BYTES = 56 * 1024 * 1024

ROW_TILE = 1024
FFN_TILE = 256
FFN_ROW_SPLIT = 2
PROMPT_CHUNK = 64
PROMPT_CHUNKS_PER_STEP = 8
SWA_BLOCKS_PER_STEP = 2
SAMPLE_SEQS_PER_STEP = 8
SAMPLE_SEQS_PER_SCAN_STEP = 4


def _rms(x, g):
    return x * lax.rsqrt(jnp.mean(x * x, axis=-1, keepdims=True) + EPS) * g


def _sigmoid(x):
    return 1.0 / (1.0 + jnp.exp(-x))


def _cparams(*sem):
    return pltpu.CompilerParams(dimension_semantics=sem, vmem_limit_bytes=VMEM_LIMIT_BYTES)


def _lead_tail_specs(n_lead_tiles, cols, row_tile=ROW_TILE):
    return [pl.BlockSpec((row_tile, cols), lambda i: (jnp.minimum(i, n_lead_tiles - 1), 0)),
            pl.BlockSpec((row_tile, cols), lambda i: (jnp.maximum(i - n_lead_tiles, 0), 0))]


def _ffn_body(*refs, n_src, n_dst, n_lead_tiles, row_split, chunk_cols):
    refs = list(refs)
    x_refs = [refs.pop(0) for _ in range(n_src)]
    gpre_ref, gpost_ref, win_ref, wout_ref = [refs.pop(0) for _ in range(4)]
    out_refs = [refs.pop(0) for _ in range(n_dst)]
    h_ref, acc_ref = refs.pop(0), refs.pop(0)
    i = pl.program_id(0)
    starts = list(range(0, D_FF, chunk_cols))
    nf = len(starts)
    rows = h_ref.shape[0] // row_split

    if n_src == 2:
        x_ref = refs.pop(0)

        @pl.when(i < n_lead_tiles)
        def _():
            x_ref[...] = x_refs[0][...]

        @pl.when(i >= n_lead_tiles)
        def _():
            x_ref[...] = x_refs[1][...]
    else:
        x_ref = x_refs[0]
    out_ref = out_refs[-1]

    for f, c0 in enumerate(starts):
        c1 = min(c0 + chunk_cols, D_FF)
        wg = win_ref[:, c0:c1]
        wu = win_ref[:, D_FF + c0:D_FF + c1]
        wo = wout_ref[c0:c1, :]
        for r in range(row_split):
            rs = slice(r * rows, (r + 1) * rows)
            if f == 0:
                h = _rms(x_ref[rs, :], gpre_ref[...]).astype(BF16)
                h_ref[rs, :] = h
            else:
                h = h_ref[rs, :]
            gate = jnp.dot(h, wg, preferred_element_type=F32)
            up = jnp.dot(h, wu, preferred_element_type=F32)
            act = (gate * _sigmoid(gate) * up).astype(BF16)
            part = jnp.dot(act, wo, preferred_element_type=F32)
            if f == 0:
                acc_ref[rs, :] = part
            elif f == nf - 1:
                out_ref[rs, :] = x_ref[rs, :] + 0.5 * _rms(acc_ref[rs, :] + part, gpost_ref[...])
            else:
                acc_ref[rs, :] += part

    if n_dst == 2:
        @pl.when(i < n_lead_tiles)
        def _():
            out_refs[0][...] = out_ref[...]


def _ffn(xs, g_pre, g_post, w_in, w_out, layer, which, split_rows=None, row_tile=ROW_TILE,
         row_split=FFN_ROW_SPLIT, chunk_cols=FFN_TILE):
    xs = xs if isinstance(xs, (tuple, list)) else (xs,)
    n = sum(x.shape[0] for x in xs)
    assert D_FF > chunk_cols
    n_lead = xs[0].shape[0] if len(xs) == 2 else (split_rows[0] if split_rows else n)
    assert split_rows is None or len(xs) == 1 or split_rows[0] == xs[0].shape[0]
    n_lead_tiles = n_lead // row_tile
    resident = pl.Buffered(1)
    row_spec = pl.BlockSpec((row_tile, D_MODEL), lambda i: (i, 0))
    x_specs = _lead_tail_specs(n_lead_tiles, D_MODEL, row_tile) if len(xs) == 2 else [row_spec]
    scratch = [pltpu.VMEM((row_tile, D_MODEL), BF16), pltpu.VMEM((row_tile, D_MODEL), F32)]
    if len(xs) == 2:
        scratch.append(pltpu.VMEM((row_tile, D_MODEL), F32))
    if split_rows:
        out_specs = _lead_tail_specs(n_lead_tiles, D_MODEL, row_tile)
        out_shape = [jax.ShapeDtypeStruct((r, D_MODEL), F32) for r in split_rows]
    else:
        out_specs = row_spec
        out_shape = jax.ShapeDtypeStruct((n, D_MODEL), F32)
    return pl.pallas_call(
        functools.partial(_ffn_body, n_src=len(xs), n_dst=2 if split_rows else 1, n_lead_tiles=n_lead_tiles,
                          row_split=row_split, chunk_cols=chunk_cols),
        grid=(n // row_tile,),
        in_specs=x_specs + [
            pl.BlockSpec((1, D_MODEL), lambda i: (0, 0)),
            pl.BlockSpec((1, D_MODEL), lambda i: (0, 0)),
            pl.BlockSpec((None, None, D_MODEL, 2 * D_FF), lambda i: (layer, which, 0, 0), pipeline_mode=resident),
            pl.BlockSpec((None, None, D_FF, D_MODEL), lambda i: (layer, which, 0, 0), pipeline_mode=resident),
        ],
        out_specs=out_specs,
        out_shape=out_shape,
        scratch_shapes=scratch,
        compiler_params=_cparams("arbitrary"),
        name="ffn",
    )(*xs, g_pre, g_post, w_in, w_out)


def _pre_mm_body(x_ref, g_ref, w_ref, b_ref, out_ref, h_ref):
    @pl.when(pl.program_id(1) == 0)
    def _():
        h_ref[...] = _rms(x_ref[...], g_ref[...]).astype(BF16)

    out_ref[...] = jnp.dot(h_ref[...], w_ref[...], preferred_element_type=F32) + b_ref[...]


def _pre_mm(x, g, w, b, name):
    n = x.shape[0]
    n_out = w.shape[1]
    tn = min(n_out, 1024)
    return pl.pallas_call(
        _pre_mm_body,
        grid=(n // ROW_TILE, n_out // tn),
        in_specs=[
            pl.BlockSpec((ROW_TILE, D_MODEL), lambda i, j: (i, 0)),
            pl.BlockSpec((1, D_MODEL), lambda i, j: (0, 0)),
            pl.BlockSpec((D_MODEL, tn), lambda i, j: (0, j)),
            pl.BlockSpec((1, tn), lambda i, j: (0, j)),
        ],
        out_specs=pl.BlockSpec((ROW_TILE, tn), lambda i, j: (i, j)),
        out_shape=jax.ShapeDtypeStruct((n, n_out), F32),
        scratch_shapes=[pltpu.VMEM((ROW_TILE, D_MODEL), BF16)],
        compiler_params=_cparams("parallel", "arbitrary"),
        name=name,
    )(x, g, w, b)


def _hgrn_in_body(x_ref, g_ref, w_ref, z_ref, qig_ref):
    rows = ROW_TILE // 2
    d = D_MODEL
    for r in range(2):
        rs = slice(r * rows, (r + 1) * rows)
        h = _rms(x_ref[rs, :], g_ref[...]).astype(BF16)
        z_ref[rs, :] = jnp.dot(h, w_ref[:, d:2 * d], preferred_element_type=F32)
        for j, c0 in enumerate((0, 2 * d, 3 * d)):
            qig_ref[rs, j * d:(j + 1) * d] = jnp.dot(h, w_ref[:, c0:c0 + d],
                                                     preferred_element_type=F32).astype(BF16)


def _hgrn_in(x, g, w, layer):
    n = x.shape[0]
    return pl.pallas_call(
        _hgrn_in_body,
        grid=(n // ROW_TILE,),
        in_specs=[
            pl.BlockSpec((ROW_TILE, D_MODEL), lambda i: (i, 0)),
            pl.BlockSpec((1, D_MODEL), lambda i: (0, 0)),
            pl.BlockSpec((None, D_MODEL, 4 * D_MODEL), lambda i: (layer, 0, 0), pipeline_mode=pl.Buffered(1)),
        ],
        out_specs=[
            pl.BlockSpec((ROW_TILE, D_MODEL), lambda i: (i, 0)),
            pl.BlockSpec((ROW_TILE, 3 * D_MODEL), lambda i: (i, 0)),
        ],
        out_shape=[
            jax.ShapeDtypeStruct((n, D_MODEL), F32),
            jax.ShapeDtypeStruct((n, 3 * D_MODEL), BF16),
        ],
        compiler_params=_cparams("parallel"),
        name="hgrn_in",
    )(x, g, w)


def _post_mm_body(o_lead_ref, o_tail_ref, w_ref, b_ref, g_ref, x_ref, out_ref, *, n_lead_tiles):
    i = pl.program_id(0)

    def run(o_ref):
        m = jnp.dot(o_ref[...], w_ref[...], preferred_element_type=F32) + b_ref[...]
        out_ref[...] = x_ref[...] + _rms(m, g_ref[...])

    pl.when(i < n_lead_tiles)(functools.partial(run, o_lead_ref))
    pl.when(i >= n_lead_tiles)(functools.partial(run, o_tail_ref))


def _post_mm(o_lead, o_tail, w, b, g, x, name):
    n = x.shape[0]
    n_lead_tiles = o_lead.shape[0] // ROW_TILE
    return pl.pallas_call(
        functools.partial(_post_mm_body, n_lead_tiles=n_lead_tiles),
        grid=(n // ROW_TILE,),
        in_specs=_lead_tail_specs(n_lead_tiles, D_MODEL) + [
            pl.BlockSpec((D_MODEL, D_MODEL), lambda i: (0, 0)),
            pl.BlockSpec((1, D_MODEL), lambda i: (0, 0)),
            pl.BlockSpec((1, D_MODEL), lambda i: (0, 0)),
            pl.BlockSpec((ROW_TILE, D_MODEL), lambda i: (i, 0)),
        ],
        out_specs=pl.BlockSpec((ROW_TILE, D_MODEL), lambda i: (i, 0)),
        out_shape=jax.ShapeDtypeStruct((n, D_MODEL), F32),
        compiler_params=_cparams("parallel"),
        name=name,
    )(o_lead, o_tail, w, b, g, x)


def _split3(a):
    a1 = a.astype(BF16)
    r1 = a - a1.astype(F32)
    a2 = r1.astype(BF16)
    a3 = (r1 - a2.astype(F32)).astype(BF16)
    return a1, a2, a3


def _hgrn_body(*refs, chunk, subs, seqs, layer, has_init, has_prev_out):
    refs = list(refs)
    q_ref, z_ref, i_ref, g_ref, loglb_ref, log1mlb_ref, onemlb_ref, gn_ref = refs[:8]
    del refs[:8]
    s0_ref = refs.pop(0) if has_init else None
    if has_prev_out:
        refs.pop(0)
    o_ref, sout_ref, st_ref = refs
    c = pl.program_id(1)
    n_chunks = pl.num_programs(1)

    @pl.when(c == 0)
    def _():
        if has_init:
            for s in range(seqs):
                for h in range(A_HEADS):
                    st_ref[s * A_HEADS + h] = s0_ref[s, h].T
        else:
            st_ref[...] = jnp.zeros_like(st_ref)

    row = lax.broadcasted_iota(jnp.int32, (chunk, chunk), 0)
    col = lax.broadcasted_iota(jnp.int32, (chunk, chunk), 1)
    causal = col <= row
    tril = causal.astype(BF16)
    mid = chunk // 2
    gn = gn_ref[...]
    log_lb = loglb_ref[...]
    log1m_lb = log1mlb_ref[...]
    onem_lb = onemlb_ref[...]
    dn_t = (((1,), (1,)), ((), ()))
    dn_0 = (((0,), (0,)), ((), ()))

    for s, u in [(s, u) for s in range(seqs) for u in range(subs)]:
        r0 = (s * subs + u) * chunk
        rs = slice(r0, r0 + chunk)
        q = q_ref[rs, :].astype(F32)
        z = z_ref[rs, :]
        g = g_ref[rs, :].astype(F32)
        vb = i_ref[rs, :]

        e = jnp.exp(-jnp.abs(z))
        r = 1.0 / (1.0 + e)
        log_sig = jnp.minimum(z, 0.0) - jnp.log(1.0 + e)
        b_ = log1m_lb + log_sig
        log_f = jnp.maximum(log_lb, b_) + jnp.log(1.0 + jnp.exp(-jnp.abs(log_lb - b_)))
        k = onem_lb * jnp.where(z >= 0.0, e * r, r)

        f1, f2, f3 = _split3(log_f)
        cum = (jnp.dot(tril, f1, preferred_element_type=F32)
               + jnp.dot(tril, f2, preferred_element_type=F32)
               + jnp.dot(tril, f3, preferred_element_type=F32))
        last = cum[chunk - 1:chunk, :]
        ref_row = cum[mid:mid + 1, :]

        dec = jnp.exp(cum - ref_row)
        q_dec = q * dec
        k_dec = k * (1.0 / dec)
        qs = q_dec.astype(BF16)
        ks = k_dec.astype(BF16)
        qe = (q_dec * jnp.exp(ref_row)).astype(BF16)
        ke = (k_dec * jnp.exp(last - ref_row)).astype(BF16)
        e_last = jnp.exp(last)

        lanes = [slice(h * A_DK, (h + 1) * A_DK) for h in range(A_HEADS)]
        scores = [lax.dot_general(qs[:, sl], ks[:, sl], dn_t, preferred_element_type=F32) for sl in lanes]
        sts = [st_ref[s * A_HEADS + h] for h in range(A_HEADS)]
        inter = [lax.dot_general(qe[:, sl], st.astype(BF16), dn_t, preferred_element_type=F32)
                 for sl, st in zip(lanes, sts)]
        kv = [lax.dot_general(vb[:, sl], ke[:, sl], dn_0, preferred_element_type=F32) for sl in lanes]
        probs = [jnp.where(causal, sc, 0.0).astype(BF16) for sc in scores]
        outs = [jnp.dot(p, vb[:, sl], preferred_element_type=F32) + it
                for p, sl, it in zip(probs, lanes, inter)]
        for h, sl in enumerate(lanes):
            st_ref[s * A_HEADS + h] = e_last[:, sl] * sts[h] + kv[h]
        o = jnp.concatenate([_rms(o_h, gn) for o_h in outs], axis=-1)
        o_ref[rs, :] = (o * (g * _sigmoid(g))).astype(o_ref.dtype)

    @pl.when(c == n_chunks - 1)
    def _():
        own = sout_ref if has_prev_out else sout_ref.at[layer]
        for s in range(seqs):
            for h in range(A_HEADS):
                own[s, h] = st_ref[s * A_HEADS + h].T
        if not has_prev_out:
            for other in range(N_A_LAYERS):
                if other != layer:
                    sout_ref[other] = jnp.zeros(sout_ref.shape[1:], sout_ref.dtype)


def _hgrn_scan(z, qig, lb_terms, gn, s0, s_prev, *, layer, row0, n_seq, seq_len, chunk, subs, seqs, name):
    n_chunks = seq_len // (subs * chunk)
    blk = seqs * subs * chunk
    assert n_chunks == 1 or seqs == 1
    blk0 = row0 // blk
    has_init = s0 is not None
    has_prev = s_prev is not None

    def rows_at(j):
        return pl.BlockSpec((blk, D_MODEL), lambda b, c: (blk0 + b * n_chunks + c, j))

    in_specs = [rows_at(0), rows_at(0), rows_at(1), rows_at(2)]
    in_specs += [pl.BlockSpec((1, D_MODEL), lambda b, c: (0, 0))] * 3
    in_specs += [pl.BlockSpec((1, A_DV), lambda b, c: (0, 0))]
    args = [qig, z, qig, qig, *lb_terms, gn]
    if has_init:
        in_specs.append(pl.BlockSpec((None, seqs, A_HEADS, A_DK, A_DV), lambda b, c: (layer, b, 0, 0, 0)))
        args.append(s0)
    aliases = {}
    if has_prev:
        in_specs.append(pl.BlockSpec(memory_space=pl.ANY))
        aliases = {len(args): 1}
        args.append(s_prev)
    if has_prev:
        state_spec = pl.BlockSpec((None, seqs, A_HEADS, A_DK, A_DV), lambda b, c: (layer, b, 0, 0, 0))
    else:
        state_spec = pl.BlockSpec((N_A_LAYERS, seqs, A_HEADS, A_DK, A_DV), lambda b, c: (0, b, 0, 0, 0))
    return pl.pallas_call(
        functools.partial(_hgrn_body, chunk=chunk, subs=subs, seqs=seqs, layer=layer, has_init=has_init,
                          has_prev_out=has_prev),
        grid=(n_seq // seqs, n_chunks),
        in_specs=in_specs,
        out_specs=[pl.BlockSpec((blk, D_MODEL), lambda b, c: (b * n_chunks + c, 0)), state_spec],
        out_shape=[
            jax.ShapeDtypeStruct((n_seq * seq_len, D_MODEL), BF16),
            jax.ShapeDtypeStruct((N_A_LAYERS, n_seq, A_HEADS, A_DK, A_DV), F32),
        ],
        scratch_shapes=[pltpu.VMEM((seqs * A_HEADS, A_DV, A_DK), F32)],
        input_output_aliases=aliases,
        compiler_params=_cparams("parallel", "arbitrary"),
        name=name,
    )(*args)


def _swa_prompt_body(sink_ref, q_ref, kvp_ref, kvc_ref, o_ref):
    i = pl.program_id(1)
    nq = q_ref.shape[0] // WINDOW
    key = lax.broadcasted_iota(jnp.int32, (2 * WINDOW, WINDOW), 0)
    qry = lax.broadcasted_iota(jnp.int32, (2 * WINDOW, WINDOW), 1)
    band = ((key < WINDOW) & (key >= qry)) | ((key >= WINDOW) & (key - WINDOW <= qry))
    first_band = band & ((key >= WINDOW) | (i > 0))
    scale = B_HEAD_DIM ** -0.5
    kv_w = B_KV_HEADS * B_HEAD_DIM
    dn_t = (((1,), (1,)), ((), ()))
    dn_0 = (((0,), (0,)), ((), ()))
    pairs = [(j, kh) for j in range(nq) for kh in range(B_KV_HEADS)]

    kv_rows = jnp.concatenate([kvp_ref[...], kvc_ref[...]], axis=0).astype(BF16)
    scores = []
    for j, kh in pairs:
        k = kv_rows[j * WINDOW:(j + 2) * WINDOW, kh * B_HEAD_DIM:(kh + 1) * B_HEAD_DIM]
        heads = range(kh * B_GROUP, (kh + 1) * B_GROUP)
        q = jnp.concatenate([q_ref[j * WINDOW:(j + 1) * WINDOW, h * B_HEAD_DIM:(h + 1) * B_HEAD_DIM]
                             for h in heads], axis=0)
        q = (q * scale).astype(BF16)
        scores.append(lax.dot_general(k, q, dn_t, preferred_element_type=F32))
    probs = []
    for (j, kh), s_t in zip(pairs, scores):
        mask = first_band if j == 0 else band
        p_parts, inv_parts = [], []
        for gi in range(B_GROUP):
            s = jnp.where(mask, s_t[:, gi * WINDOW:(gi + 1) * WINDOW], NEG_INF)
            sink = sink_ref[kh * B_GROUP + gi]
            m = jnp.maximum(jnp.max(s, axis=0, keepdims=True), sink)
            p = jnp.exp(s - m)
            inv_parts.append(1.0 / (jnp.sum(p, axis=0, keepdims=True) + jnp.exp(sink - m)))
            p_parts.append(p.astype(BF16))
        probs.append((jnp.concatenate(p_parts, axis=1), jnp.concatenate(inv_parts, axis=1)))
    outs = {}
    for (j, kh), (p_t, inv) in zip(pairs, probs):
        v = kv_rows[j * WINDOW:(j + 2) * WINDOW, kv_w + kh * B_HEAD_DIM:kv_w + (kh + 1) * B_HEAD_DIM]
        outs[j, kh] = lax.dot_general(v, p_t, dn_0, preferred_element_type=F32) * inv
    for j in range(nq):
        o_t = jnp.concatenate([outs[j, kh][:, gi * WINDOW:(gi + 1) * WINDOW]
                               for kh in range(B_KV_HEADS) for gi in range(B_GROUP)], axis=0)
        o_ref[j * WINDOW:(j + 1) * WINDOW, :] = o_t.T.astype(o_ref.dtype)


def _swa_prompt(q, kv, sinks, *, n_seq, seq_len):
    nq = math.gcd(seq_len // WINDOW, SWA_BLOCKS_PER_STEP)
    rows = nq * WINDOW
    nb = seq_len // rows
    kv_cols = kv.shape[1]
    return pl.pallas_call(
        _swa_prompt_body,
        grid=(n_seq, nb),
        in_specs=[
            pl.BlockSpec(memory_space=pltpu.SMEM),
            pl.BlockSpec((rows, D_MODEL), lambda b, i: (b * nb + i, 0)),
            pl.BlockSpec((WINDOW, kv_cols), lambda b, i: (jnp.maximum((b * nb + i) * nq - 1, 0), 0)),
            pl.BlockSpec((rows, kv_cols), lambda b, i: (b * nb + i, 0)),
        ],
        out_specs=pl.BlockSpec((rows, D_MODEL), lambda b, i: (b * nb + i, 0)),
        out_shape=jax.ShapeDtypeStruct((n_seq * seq_len, D_MODEL), BF16),
        compiler_params=_cparams("parallel", "arbitrary"),
        name="swa_prompt",
    )(sinks, q, kv, kv)


def _swa_sample_body(sink_ref, q_ref, kc_ref, vc_ref, kvn_ref, o_ref, *, t_new):
    rows = B_GROUP * t_new
    t_row = lax.broadcasted_iota(jnp.int32, (rows, WINDOW), 0) % t_new
    c_col = lax.broadcasted_iota(jnp.int32, (rows, WINDOW), 1)
    mask_cache = c_col >= t_row
    t_row_n = lax.broadcasted_iota(jnp.int32, (rows, t_new), 0) % t_new
    t_col_n = lax.broadcasted_iota(jnp.int32, (rows, t_new), 1)
    mask_new = t_col_n <= t_row_n
    scale = B_HEAD_DIM ** -0.5
    kv_w = B_KV_HEADS * B_HEAD_DIM
    dn = (((1,), (1,)), ((), ()))
    pairs = [(s, kh) for s in range(SAMPLE_SEQS_PER_STEP) for kh in range(B_KV_HEADS)]
    scores = []
    for s, kh in pairs:
        ksl = slice(kh * B_HEAD_DIM, (kh + 1) * B_HEAD_DIM)
        q = (q_ref[s, kh] * scale).astype(BF16)
        s_c = lax.dot_general(q, kc_ref[s, :, ksl].astype(BF16), dn, preferred_element_type=F32)
        s_n = lax.dot_general(q, kvn_ref[s, :, ksl].astype(BF16), dn, preferred_element_type=F32)
        scores.append((s_c, s_n))
    probs = []
    for (s, kh), (s_c, s_n) in zip(pairs, scores):
        s_c = jnp.where(mask_cache, s_c, NEG_INF)
        s_n = jnp.where(mask_new, s_n, NEG_INF)
        sink = sink_ref[kh]
        m = jnp.maximum(jnp.maximum(jnp.max(s_c, axis=-1, keepdims=True),
                                    jnp.max(s_n, axis=-1, keepdims=True)), sink)
        p_c = jnp.exp(s_c - m)
        p_n = jnp.exp(s_n - m)
        denom = (jnp.sum(p_c, axis=-1, keepdims=True) + jnp.sum(p_n, axis=-1, keepdims=True)
                 + jnp.exp(sink - m))
        probs.append((p_c.astype(BF16), p_n.astype(BF16), 1.0 / denom))
    for (s, kh), (p_c, p_n, inv) in zip(pairs, probs):
        ksl = slice(kh * B_HEAD_DIM, (kh + 1) * B_HEAD_DIM)
        vsl = slice(kv_w + kh * B_HEAD_DIM, kv_w + (kh + 1) * B_HEAD_DIM)
        pv = (jnp.dot(p_c, vc_ref[s, :, ksl].astype(BF16), preferred_element_type=F32)
              + jnp.dot(p_n, kvn_ref[s, :, vsl].astype(BF16), preferred_element_type=F32))
        o_ref[s, kh] = (pv * inv).astype(o_ref.dtype)


def _swa_sample(q, cache_k, cache_v, kv_new, sink_rows, *, t_new):
    n_seq = q.shape[0]
    rows = B_GROUP * t_new
    sb = SAMPLE_SEQS_PER_STEP
    kv_w = B_KV_HEADS * B_HEAD_DIM
    return pl.pallas_call(
        functools.partial(_swa_sample_body, t_new=t_new),
        grid=(n_seq // sb,),
        in_specs=[
            pl.BlockSpec((B_KV_HEADS, rows, 1), lambda b: (0, 0, 0)),
            pl.BlockSpec((sb, B_KV_HEADS, rows, B_HEAD_DIM), lambda b: (b, 0, 0, 0)),
            pl.BlockSpec((sb, WINDOW, kv_w), lambda b: (b, 0, 0)),
            pl.BlockSpec((sb, WINDOW, kv_w), lambda b: (b, 0, 0)),
            pl.BlockSpec((sb, t_new, 2 * kv_w), lambda b: (b, 0, 0)),
        ],
        out_specs=pl.BlockSpec((sb, B_KV_HEADS, rows, B_HEAD_DIM), lambda b: (b, 0, 0, 0)),
        out_shape=jax.ShapeDtypeStruct((n_seq, B_KV_HEADS, rows, B_HEAD_DIM), BF16),
        compiler_params=_cparams("parallel"),
        name="swa_sample",
    )(sink_rows, q, cache_k, cache_v, kv_new)


def kernel(x_prompt, x_sample, state_hgrn, cache_k_window, cache_v_window, norm_g, ffn_w_in, ffn_w_out,
           a_w_in, a_lower_bounds, a_gnorm, a_w_out, kv_norm, kv_w, kv_b, b_w_q, b_b_q, b_sinks, b_w_o, b_b_o):
    n_pb, p_len, _ = x_prompt.shape
    n_sb, s_len, _ = x_sample.shape
    n_p = n_pb * p_len
    n_s = n_sb * s_len
    kv_cols = B_KV_HEADS * B_HEAD_DIM

    x = (x_prompt.reshape(n_p, D_MODEL), x_sample.reshape(n_s, D_MODEL))

    ffn_w_in_b = ffn_w_in.astype(BF16)
    ffn_w_out_b = ffn_w_out.astype(BF16)
    a_w_in_b = a_w_in.astype(BF16)
    a_w_out_b = a_w_out.astype(BF16)
    kv_w_b = kv_w.astype(BF16)
    b_w_q_b = b_w_q.astype(BF16)
    b_w_o_b = b_w_o.astype(BF16)
    zero_bias = jnp.zeros((1, D_MODEL), F32)

    lb_soft = jax.nn.softmax(a_lower_bounds.astype(F32), axis=0)
    lbs = jnp.cumsum(lb_soft, axis=0) - lb_soft[0]
    pos = lbs > 0
    log_lb = jnp.where(pos, jnp.log(jnp.where(pos, lbs, 1.0)), NEG_INF)
    log1m_lb = jnp.log1p(-lbs)
    onem_lb = 1.0 - lbs

    cache_k = cache_k_window.reshape(n_sb, WINDOW, kv_cols)
    cache_v = cache_v_window.reshape(n_sb, WINDOW, kv_cols)

    def g_row(l, j):
        return norm_g[l, j].reshape(1, D_MODEL)

    ffn_cfg = {
        (0, 0): dict(),
        (0, 1): dict(row_split=1),
        (1, 0): dict(chunk_cols=512),
        (1, 1): dict(row_split=1, chunk_cols=512),
        (2, 0): dict(row_tile=512, row_split=1),
        (2, 1): dict(row_tile=512, row_split=2),
        (3, 0): dict(),
        (3, 1): dict(),
    }
    s_prompt = s_sample = None
    kv = None
    for l in range(DEPTH):
        x = _ffn(x, g_row(l, 0), g_row(l, 1), ffn_w_in_b, ffn_w_out_b, l, 0, **ffn_cfg[l, 0])
        if l < N_A_LAYERS:
            z, qig = _hgrn_in(x, g_row(l, 2), a_w_in_b, l)
            lb_terms = (log_lb[l].reshape(1, D_MODEL), log1m_lb[l].reshape(1, D_MODEL),
                        onem_lb[l].reshape(1, D_MODEL))
            gn = a_gnorm[l].reshape(1, A_DV)
            p_chunk = math.gcd(p_len, PROMPT_CHUNK)
            o_p, s_prompt = _hgrn_scan(z, qig, lb_terms, gn, None, s_prompt, layer=l, row0=0, n_seq=n_pb,
                                       seq_len=p_len, chunk=p_chunk,
                                       subs=math.gcd(p_len // p_chunk, PROMPT_CHUNKS_PER_STEP), seqs=1,
                                       name="hgrn_prompt")
            o_s, s_sample = _hgrn_scan(z, qig, lb_terms, gn, state_hgrn, s_sample, layer=l, row0=n_p,
                                       n_seq=n_sb, seq_len=s_len, chunk=s_len, subs=1,
                                       seqs=SAMPLE_SEQS_PER_SCAN_STEP, name="hgrn_sample")
            x = _post_mm(o_p, o_s, a_w_out_b[l], zero_bias, g_row(l, 3), x, "hgrn_out")
        else:
            j = l - N_A_LAYERS
            q = _pre_mm(x, g_row(l, 2), b_w_q_b[j], b_b_q[j].reshape(1, D_MODEL), "swa_q")
            o_p = _swa_prompt(q, kv, b_sinks[j], n_seq=n_pb, seq_len=p_len)
            q_s = q[n_p:].reshape(n_sb, s_len, B_KV_HEADS, B_GROUP, B_HEAD_DIM)
            q_s = q_s.transpose(0, 2, 3, 1, 4).reshape(n_sb, B_KV_HEADS, B_GROUP * s_len, B_HEAD_DIM)
            sink_rows = jnp.repeat(b_sinks[j].reshape(B_KV_HEADS, B_GROUP), s_len, axis=1)[..., None]
            o_s = _swa_sample(q_s, cache_k, cache_v, kv[n_p:].reshape(n_sb, s_len, 2 * kv_cols),
                              sink_rows, t_new=s_len)
            o_s = o_s.reshape(n_sb, B_KV_HEADS, B_GROUP, s_len, B_HEAD_DIM)
            o_s = o_s.transpose(0, 3, 1, 2, 4).reshape(n_s, D_MODEL)
            x = _post_mm(o_p, o_s, b_w_o_b[j], b_b_o[j].reshape(1, D_MODEL), g_row(l, 3), x, "swa_out")
        x = _ffn(x, g_row(l, 4), g_row(l, 5), ffn_w_in_b, ffn_w_out_b, l, 1,
                 split_rows=(n_p, n_s) if l == DEPTH - 1 else None, **ffn_cfg[l, 1])
        if l == N_A_LAYERS - 1:
            kv = _pre_mm(x, kv_norm.reshape(1, D_MODEL), kv_w_b, kv_b.reshape(1, 2 * kv_cols), "shared_kv")

    y_prompt = x[0].reshape(n_pb, p_len, D_MODEL)
    y_sample = x[1].reshape(n_sb, s_len, D_MODEL)

    kv_p = jnp.stack([kv[(b + 1) * p_len - WINDOW:(b + 1) * p_len] for b in range(n_pb)])
    kv_p = kv_p.reshape(n_pb, WINDOW, 2, B_KV_HEADS, B_HEAD_DIM)
    kv_s = kv[n_p:].reshape(n_sb, s_len, 2, B_KV_HEADS, B_HEAD_DIM)
    k_win_sample = jnp.concatenate([cache_k_window, kv_s[:, :, 0]], axis=1)[:, -WINDOW:]
    v_win_sample = jnp.concatenate([cache_v_window, kv_s[:, :, 1]], axis=1)[:, -WINDOW:]
    return (y_prompt, y_sample, s_prompt, s_sample,
            kv_p[:, :, 0], kv_p[:, :, 1], k_win_sample, v_win_sample)
```

```python
import functools
import math

import jax
import jax.numpy as jnp
from jax import lax
from jax.experimental import pallas as pl
from jax.experimental.pallas import tpu as pltpu

F32 = jnp.float32
BF16 = jnp.bfloat16

D_MODEL = 1024
DEPTH = 4
N_A_LAYERS = 2
A_HEADS = 8
A_DK = 128
A_DV = 128
B_HEAD_DIM = 64
B_HEADS = 16
B_KV_HEADS = 2
B_GROUP = 8
WINDOW = 128
D_FF = 2816
EPS = 1e-6
NEG_INF = -1e30

VMEM_LIMIT_BYTES = 56 * 1024 * 1024

ROW_TILE = 1024
FFN_TILE = 256
FFN_ROW_SPLIT = 2
WEIGHT_CAST_BLOCKS = 16
PROMPT_CHUNK = 64
PROMPT_CHUNKS_PER_STEP = 8
SWA_BLOCKS_PER_STEP = 4
SAMPLE_SEQS_PER_STEP = 8
SAMPLE_SEQS_PER_SCAN_STEP = 4


def _rms(x, g):
    return x * lax.rsqrt(jnp.mean(x * x, axis=-1, keepdims=True) + EPS) * g


def _sigmoid(x):
    return 1.0 / (1.0 + jnp.exp(-x))


def _cparams(*sem):
    return pltpu.CompilerParams(dimension_semantics=sem, vmem_limit_bytes=VMEM_LIMIT_BYTES)


def _lead_tail_specs(n_lead_tiles, cols, row_tile=ROW_TILE):
    return [pl.BlockSpec((row_tile, cols), lambda i: (jnp.minimum(i, n_lead_tiles - 1), 0)),
            pl.BlockSpec((row_tile, cols), lambda i: (jnp.maximum(i - n_lead_tiles, 0), 0))]


def _ffn_body(*refs, n_src, n_dst, n_lead_tiles, cast_next):
    refs = list(refs)
    x_refs = [refs.pop(0) for _ in range(n_src)]
    gpre_ref, gpost_ref, win_ref, wout_ref = [refs.pop(0) for _ in range(4)]
    next_f32 = [refs.pop(0) for _ in range(2)] if cast_next else []
    out_refs = [refs.pop(0) for _ in range(n_dst)]
    next_bf16 = [refs.pop(0) for _ in range(2)] if cast_next else []
    h_ref = refs.pop(0)
    i = pl.program_id(0)
    nf = D_FF // FFN_TILE
    rows = ROW_TILE // FFN_ROW_SPLIT

    for src, dst in zip(next_f32, next_bf16):
        dst[...] = src[...].astype(BF16)

    if n_src == 2:
        x_ref = refs.pop(0)

        @pl.when(i < n_lead_tiles)
        def _():
            x_ref[...] = x_refs[0][...]

        @pl.when(i >= n_lead_tiles)
        def _():
            x_ref[...] = x_refs[1][...]
    else:
        x_ref = x_refs[0]
    out_ref = out_refs[-1]

    for f in range(nf):
        c0, c1 = f * FFN_TILE, (f + 1) * FFN_TILE
        wg = win_ref[:, c0:c1]
        wu = win_ref[:, D_FF + c0:D_FF + c1]
        wo = wout_ref[c0:c1, :]
        for r in range(FFN_ROW_SPLIT):
            rs = slice(r * rows, (r + 1) * rows)
            if f == 0:
                h = _rms(x_ref[rs, :], gpre_ref[...]).astype(BF16)
                h_ref[rs, :] = h
            else:
                h = h_ref[rs, :]
            gate = jnp.dot(h, wg, preferred_element_type=F32)
            up = jnp.dot(h, wu, preferred_element_type=F32)
            act = (gate * _sigmoid(gate) * up).astype(BF16)
            part = jnp.dot(act, wo, preferred_element_type=F32)
            if f == 0:
                out_ref[rs, :] = part
            elif f == nf - 1:
                out_ref[rs, :] = x_ref[rs, :] + 0.5 * _rms(out_ref[rs, :] + part, gpost_ref[...])
            else:
                out_ref[rs, :] += part

    if n_dst == 2:
        @pl.when(i < n_lead_tiles)
        def _():
            out_refs[0][...] = out_ref[...]


def _ffn(xs, g_pre, g_post, w_in, w_out, split_rows=None, cast_next=None):
    xs = xs if isinstance(xs, (tuple, list)) else (xs,)
    n = sum(x.shape[0] for x in xs)
    n_tiles = n // ROW_TILE
    assert D_FF // FFN_TILE >= 2
    n_lead = xs[0].shape[0] if len(xs) == 2 else (split_rows[0] if split_rows else n)
    assert split_rows is None or len(xs) == 1 or split_rows[0] == xs[0].shape[0]
    n_lead_tiles = n_lead // ROW_TILE
    resident = pl.Buffered(1)
    row_spec = pl.BlockSpec((ROW_TILE, D_MODEL), lambda i: (i, 0))
    in_specs = _lead_tail_specs(n_lead_tiles, D_MODEL) if len(xs) == 2 else [row_spec]
    in_specs += [
        pl.BlockSpec((1, D_MODEL), lambda i: (0, 0)),
        pl.BlockSpec((1, D_MODEL), lambda i: (0, 0)),
        pl.BlockSpec((D_MODEL, 2 * D_FF), lambda i: (0, 0), pipeline_mode=resident),
        pl.BlockSpec((D_FF, D_MODEL), lambda i: (0, 0), pipeline_mode=resident),
    ]
    args = [*xs, g_pre, g_post, w_in, w_out]
    scratch = [pltpu.VMEM((ROW_TILE, D_MODEL), BF16)]
    if len(xs) == 2:
        scratch.append(pltpu.VMEM((ROW_TILE, D_MODEL), F32))
    if split_rows:
        out_specs = _lead_tail_specs(n_lead_tiles, D_MODEL)
        out_shape = [jax.ShapeDtypeStruct((r, D_MODEL), F32) for r in split_rows]
    else:
        out_specs = [row_spec]
        out_shape = [jax.ShapeDtypeStruct((n, D_MODEL), F32)]
    n_dst = len(out_specs)
    if cast_next:
        wn_in, wn_out, layer, which = cast_next
        assert n_tiles >= WEIGHT_CAST_BLOCKS
        for w in (wn_in, wn_out):
            rows, cols = w.shape[2] // WEIGHT_CAST_BLOCKS, w.shape[3]
            assert rows * WEIGHT_CAST_BLOCKS == w.shape[2]
            in_specs.append(pl.BlockSpec(
                (None, None, rows, cols),
                lambda i: (layer, which, jnp.minimum(i, WEIGHT_CAST_BLOCKS - 1), 0)))
            out_specs.append(pl.BlockSpec((rows, cols), lambda i: (jnp.minimum(i, WEIGHT_CAST_BLOCKS - 1), 0)))
            out_shape.append(jax.ShapeDtypeStruct(w.shape[2:], BF16))
            args.append(w)
    outs = pl.pallas_call(
        functools.partial(_ffn_body, n_src=len(xs), n_dst=n_dst, n_lead_tiles=n_lead_tiles,
                          cast_next=bool(cast_next)),
        grid=(n_tiles,),
        in_specs=in_specs,
        out_specs=out_specs,
        out_shape=out_shape,
        scratch_shapes=scratch,
        compiler_params=_cparams("arbitrary"),
        name="ffn",
    )(*args)
    y = outs[:n_dst] if split_rows else outs[0]
    return y, (tuple(outs[n_dst:]) if cast_next else None)


def _pre_mm_body(x_ref, g_ref, w_ref, b_ref, out_ref, h_ref):
    @pl.when(pl.program_id(1) == 0)
    def _():
        h_ref[...] = _rms(x_ref[...], g_ref[...]).astype(BF16)

    out_ref[...] = jnp.dot(h_ref[...], w_ref[...], preferred_element_type=F32) + b_ref[...]


def _pre_mm(x, g, w, b, name):
    n = x.shape[0]
    n_out = w.shape[1]
    tn = min(n_out, 1024)
    return pl.pallas_call(
        _pre_mm_body,
        grid=(n // ROW_TILE, n_out // tn),
        in_specs=[
            pl.BlockSpec((ROW_TILE, D_MODEL), lambda i, j: (i, 0)),
            pl.BlockSpec((1, D_MODEL), lambda i, j: (0, 0)),
            pl.BlockSpec((D_MODEL, tn), lambda i, j: (0, j)),
            pl.BlockSpec((1, tn), lambda i, j: (0, j)),
        ],
        out_specs=pl.BlockSpec((ROW_TILE, tn), lambda i, j: (i, j)),
        out_shape=jax.ShapeDtypeStruct((n, n_out), F32),
        scratch_shapes=[pltpu.VMEM((ROW_TILE, D_MODEL), BF16)],
        compiler_params=_cparams("parallel", "arbitrary"),
        name=name,
    )(x, g, w, b)


def _hgrn_in_body(x_ref, g_ref, w_ref, z_ref, qig_ref):
    rows = ROW_TILE // 2
    d = D_MODEL
    for r in range(2):
        rs = slice(r * rows, (r + 1) * rows)
        h = _rms(x_ref[rs, :], g_ref[...]).astype(BF16)
        z_ref[rs, :] = jnp.dot(h, w_ref[:, d:2 * d], preferred_element_type=F32)
        for j, c0 in enumerate((0, 2 * d, 3 * d)):
            qig_ref[rs, j * d:(j + 1) * d] = jnp.dot(h, w_ref[:, c0:c0 + d],
                                                     preferred_element_type=F32).astype(BF16)


def _hgrn_in(x, g, w, layer):
    n = x.shape[0]
    return pl.pallas_call(
        _hgrn_in_body,
        grid=(n // ROW_TILE,),
        in_specs=[
            pl.BlockSpec((ROW_TILE, D_MODEL), lambda i: (i, 0)),
            pl.BlockSpec((1, D_MODEL), lambda i: (0, 0)),
            pl.BlockSpec((None, D_MODEL, 4 * D_MODEL), lambda i: (layer, 0, 0), pipeline_mode=pl.Buffered(1)),
        ],
        out_specs=[
            pl.BlockSpec((ROW_TILE, D_MODEL), lambda i: (i, 0)),
            pl.BlockSpec((ROW_TILE, 3 * D_MODEL), lambda i: (i, 0)),
        ],
        out_shape=[
            jax.ShapeDtypeStruct((n, D_MODEL), F32),
            jax.ShapeDtypeStruct((n, 3 * D_MODEL), BF16),
        ],
        compiler_params=_cparams("parallel"),
        name="hgrn_in",
    )(x, g, w)


def _post_mm_body(o_lead_ref, o_tail_ref, w_ref, b_ref, g_ref, x_ref, out_ref, *, n_lead_tiles):
    i = pl.program_id(0)

    def run(o_ref):
        m = jnp.dot(o_ref[...], w_ref[...], preferred_element_type=F32) + b_ref[...]
        out_ref[...] = x_ref[...] + _rms(m, g_ref[...])

    pl.when(i < n_lead_tiles)(functools.partial(run, o_lead_ref))
    pl.when(i >= n_lead_tiles)(functools.partial(run, o_tail_ref))


def _post_mm(o_lead, o_tail, w, b, g, x, name):
    n = x.shape[0]
    n_lead_tiles = o_lead.shape[0] // ROW_TILE
    return pl.pallas_call(
        functools.partial(_post_mm_body, n_lead_tiles=n_lead_tiles),
        grid=(n // ROW_TILE,),
        in_specs=_lead_tail_specs(n_lead_tiles, D_MODEL) + [
            pl.BlockSpec((D_MODEL, D_MODEL), lambda i: (0, 0)),
            pl.BlockSpec((1, D_MODEL), lambda i: (0, 0)),
            pl.BlockSpec((1, D_MODEL), lambda i: (0, 0)),
            pl.BlockSpec((ROW_TILE, D_MODEL), lambda i: (i, 0)),
        ],
        out_specs=pl.BlockSpec((ROW_TILE, D_MODEL), lambda i: (i, 0)),
        out_shape=jax.ShapeDtypeStruct((n, D_MODEL), F32),
        compiler_params=_cparams("parallel"),
        name=name,
    )(o_lead, o_tail, w, b, g, x)


def _split3(a):
    a1 = a.astype(BF16)
    r1 = a - a1.astype(F32)
    a2 = r1.astype(BF16)
    a3 = (r1 - a2.astype(F32)).astype(BF16)
    return a1, a2, a3


def _hgrn_body(*refs, chunk, subs, seqs, layer, has_init, has_prev_out):
    refs = list(refs)
    q_ref, z_ref, i_ref, g_ref, loglb_ref, log1mlb_ref, onemlb_ref, gn_ref = refs[:8]
    del refs[:8]
    s0_ref = refs.pop(0) if has_init else None
    if has_prev_out:
        refs.pop(0)
    o_ref, sout_ref, st_ref = refs
    c = pl.program_id(1)
    n_chunks = pl.num_programs(1)

    @pl.when(c == 0)
    def _():
        if has_init:
            for s in range(seqs):
                for h in range(A_HEADS):
                    st_ref[s * A_HEADS + h] = s0_ref[s, h].T
        else:
            st_ref[...] = jnp.zeros_like(st_ref)

    row = lax.broadcasted_iota(jnp.int32, (chunk, chunk), 0)
    col = lax.broadcasted_iota(jnp.int32, (chunk, chunk), 1)
    causal = col <= row
    tril = causal.astype(BF16)
    mid = chunk // 2
    gn = gn_ref[...]
    log_lb = loglb_ref[...]
    log1m_lb = log1mlb_ref[...]
    onem_lb = onemlb_ref[...]
    dn_t = (((1,), (1,)), ((), ()))
    dn_0 = (((0,), (0,)), ((), ()))

    for s, u in [(s, u) for s in range(seqs) for u in range(subs)]:
        r0 = (s * subs + u) * chunk
        rs = slice(r0, r0 + chunk)
        q = q_ref[rs, :].astype(F32)
        z = z_ref[rs, :]
        g = g_ref[rs, :].astype(F32)
        vb = i_ref[rs, :]

        e = jnp.exp(-jnp.abs(z))
        r = 1.0 / (1.0 + e)
        log_sig = jnp.minimum(z, 0.0) - jnp.log(1.0 + e)
        b_ = log1m_lb + log_sig
        log_f = jnp.maximum(log_lb, b_) + jnp.log(1.0 + jnp.exp(-jnp.abs(log_lb - b_)))
        k = onem_lb * jnp.where(z >= 0.0, e * r, r)

        f1, f2, f3 = _split3(log_f)
        cum = (jnp.dot(tril, f1, preferred_element_type=F32)
               + jnp.dot(tril, f2, preferred_element_type=F32)
               + jnp.dot(tril, f3, preferred_element_type=F32))
        last = cum[chunk - 1:chunk, :]
        ref_row = cum[mid:mid + 1, :]

        dec = jnp.exp(cum - ref_row)
        q_dec = q * dec
        k_dec = k * (1.0 / dec)
        qs = q_dec.astype(BF16)
        ks = k_dec.astype(BF16)
        qe = (q_dec * jnp.exp(ref_row)).astype(BF16)
        ke = (k_dec * jnp.exp(last - ref_row)).astype(BF16)
        e_last = jnp.exp(last)

        lanes = [slice(h * A_DK, (h + 1) * A_DK) for h in range(A_HEADS)]
        scores = [lax.dot_general(qs[:, sl], ks[:, sl], dn_t, preferred_element_type=F32) for sl in lanes]
        sts = [st_ref[s * A_HEADS + h] for h in range(A_HEADS)]
        inter = [lax.dot_general(qe[:, sl], st.astype(BF16), dn_t, preferred_element_type=F32)
                 for sl, st in zip(lanes, sts)]
        kv = [lax.dot_general(vb[:, sl], ke[:, sl], dn_0, preferred_element_type=F32) for sl in lanes]
        probs = [jnp.where(causal, sc, 0.0).astype(BF16) for sc in scores]
        outs = [jnp.dot(p, vb[:, sl], preferred_element_type=F32) + it
                for p, sl, it in zip(probs, lanes, inter)]
        for h, sl in enumerate(lanes):
            st_ref[s * A_HEADS + h] = e_last[:, sl] * sts[h] + kv[h]
        o = jnp.concatenate([_rms(o_h, gn) for o_h in outs], axis=-1)
        o_ref[rs, :] = (o * (g * _sigmoid(g))).astype(o_ref.dtype)

    @pl.when(c == n_chunks - 1)
    def _():
        own = sout_ref if has_prev_out else sout_ref.at[layer]
        for s in range(seqs):
            for h in range(A_HEADS):
                own[s, h] = st_ref[s * A_HEADS + h].T
        if not has_prev_out:
            for other in range(N_A_LAYERS):
                if other != layer:
                    sout_ref[other] = jnp.zeros(sout_ref.shape[1:], sout_ref.dtype)


def _hgrn_scan(z, qig, lb_terms, gn, s0, s_prev, *, layer, row0, n_seq, seq_len, chunk, subs, seqs, name):
    n_chunks = seq_len // (subs * chunk)
    blk = seqs * subs * chunk
    assert n_chunks == 1 or seqs == 1
    blk0 = row0 // blk
    has_init = s0 is not None
    has_prev = s_prev is not None

    def rows_at(j):
        return pl.BlockSpec((blk, D_MODEL), lambda b, c: (blk0 + b * n_chunks + c, j))

    in_specs = [rows_at(0), rows_at(0), rows_at(1), rows_at(2)]
    in_specs += [pl.BlockSpec((1, D_MODEL), lambda b, c: (0, 0))] * 3
    in_specs += [pl.BlockSpec((1, A_DV), lambda b, c: (0, 0))]
    args = [qig, z, qig, qig, *lb_terms, gn]
    if has_init:
        in_specs.append(pl.BlockSpec((None, seqs, A_HEADS, A_DK, A_DV), lambda b, c: (layer, b, 0, 0, 0)))
        args.append(s0)
    aliases = {}
    if has_prev:
        in_specs.append(pl.BlockSpec(memory_space=pl.ANY))
        aliases = {len(args): 1}
        args.append(s_prev)
    if has_prev:
        state_spec = pl.BlockSpec((None, seqs, A_HEADS, A_DK, A_DV), lambda b, c: (layer, b, 0, 0, 0))
    else:
        state_spec = pl.BlockSpec((N_A_LAYERS, seqs, A_HEADS, A_DK, A_DV), lambda b, c: (0, b, 0, 0, 0))
    return pl.pallas_call(
        functools.partial(_hgrn_body, chunk=chunk, subs=subs, seqs=seqs, layer=layer, has_init=has_init,
                          has_prev_out=has_prev),
        grid=(n_seq // seqs, n_chunks),
        in_specs=in_specs,
        out_specs=[pl.BlockSpec((blk, D_MODEL), lambda b, c: (b * n_chunks + c, 0)), state_spec],
        out_shape=[
            jax.ShapeDtypeStruct((n_seq * seq_len, D_MODEL), BF16),
            jax.ShapeDtypeStruct((N_A_LAYERS, n_seq, A_HEADS, A_DK, A_DV), F32),
        ],
        scratch_shapes=[pltpu.VMEM((seqs * A_HEADS, A_DV, A_DK), F32)],
        input_output_aliases=aliases,
        compiler_params=_cparams("parallel", "arbitrary"),
        name=name,
    )(*args)


def _swa_prompt_body(sink_ref, q_ref, kvp_ref, kvc_ref, o_ref):
    i = pl.program_id(1)
    nq = q_ref.shape[0] // WINDOW
    key = lax.broadcasted_iota(jnp.int32, (2 * WINDOW, WINDOW), 0)
    qry = lax.broadcasted_iota(jnp.int32, (2 * WINDOW, WINDOW), 1)
    band = ((key < WINDOW) & (key >= qry)) | ((key >= WINDOW) & (key - WINDOW <= qry))
    first_band = band & ((key >= WINDOW) | (i > 0))
    scale = B_HEAD_DIM ** -0.5
    kv_w = B_KV_HEADS * B_HEAD_DIM
    dn_t = (((1,), (1,)), ((), ()))
    dn_0 = (((0,), (0,)), ((), ()))
    pairs = [(j, kh) for j in range(nq) for kh in range(B_KV_HEADS)]

    kv_rows = jnp.concatenate([kvp_ref[...], kvc_ref[...]], axis=0).astype(BF16)
    scores = []
    for j, kh in pairs:
        k = kv_rows[j * WINDOW:(j + 2) * WINDOW, kh * B_HEAD_DIM:(kh + 1) * B_HEAD_DIM]
        heads = range(kh * B_GROUP, (kh + 1) * B_GROUP)
        q = jnp.concatenate([q_ref[j * WINDOW:(j + 1) * WINDOW, h * B_HEAD_DIM:(h + 1) * B_HEAD_DIM]
                             for h in heads], axis=0)
        q = (q * scale).astype(BF16)
        scores.append(lax.dot_general(k, q, dn_t, preferred_element_type=F32))
    probs = []
    for (j, kh), s_t in zip(pairs, scores):
        mask = first_band if j == 0 else band
        p_parts, inv_parts = [], []
        for gi in range(B_GROUP):
            s = jnp.where(mask, s_t[:, gi * WINDOW:(gi + 1) * WINDOW], NEG_INF)
            sink = sink_ref[kh * B_GROUP + gi]
            m = jnp.maximum(jnp.max(s, axis=0, keepdims=True), sink)
            p = jnp.exp(s - m)
            inv_parts.append(1.0 / (jnp.sum(p, axis=0, keepdims=True) + jnp.exp(sink - m)))
            p_parts.append(p.astype(BF16))
        probs.append((jnp.concatenate(p_parts, axis=1), jnp.concatenate(inv_parts, axis=1)))
    outs = {}
    for (j, kh), (p_t, inv) in zip(pairs, probs):
        v = kv_rows[j * WINDOW:(j + 2) * WINDOW, kv_w + kh * B_HEAD_DIM:kv_w + (kh + 1) * B_HEAD_DIM]
        outs[j, kh] = lax.dot_general(v, p_t, dn_0, preferred_element_type=F32) * inv
    for j in range(nq):
        o_t = jnp.concatenate([outs[j, kh][:, gi * WINDOW:(gi + 1) * WINDOW]
                               for kh in range(B_KV_HEADS) for gi in range(B_GROUP)], axis=0)
        o_ref[j * WINDOW:(j + 1) * WINDOW, :] = o_t.T.astype(o_ref.dtype)


def _swa_prompt(q, kv, sinks, *, n_seq, seq_len):
    nq = math.gcd(seq_len // WINDOW, SWA_BLOCKS_PER_STEP)
    rows = nq * WINDOW
    nb = seq_len // rows
    kv_cols = kv.shape[1]
    return pl.pallas_call(
        _swa_prompt_body,
        grid=(n_seq, nb),
        in_specs=[
            pl.BlockSpec(memory_space=pltpu.SMEM),
            pl.BlockSpec((rows, D_MODEL), lambda b, i: (b * nb + i, 0)),
            pl.BlockSpec((WINDOW, kv_cols), lambda b, i: (jnp.maximum((b * nb + i) * nq - 1, 0), 0)),
            pl.BlockSpec((rows, kv_cols), lambda b, i: (b * nb + i, 0)),
        ],
        out_specs=pl.BlockSpec((rows, D_MODEL), lambda b, i: (b * nb + i, 0)),
        out_shape=jax.ShapeDtypeStruct((n_seq * seq_len, D_MODEL), BF16),
        compiler_params=_cparams("parallel", "arbitrary"),
        name="swa_prompt",
    )(sinks, q, kv, kv)


def _swa_sample_body(sink_ref, q_ref, kc_ref, vc_ref, kvn_ref, o_ref, *, t_new):
    rows = B_GROUP * t_new
    t_row = lax.broadcasted_iota(jnp.int32, (rows, WINDOW), 0) % t_new
    c_col = lax.broadcasted_iota(jnp.int32, (rows, WINDOW), 1)
    mask_cache = c_col >= t_row
    t_row_n = lax.broadcasted_iota(jnp.int32, (rows, t_new), 0) % t_new
    t_col_n = lax.broadcasted_iota(jnp.int32, (rows, t_new), 1)
    mask_new = t_col_n <= t_row_n
    scale = B_HEAD_DIM ** -0.5
    kv_w = B_KV_HEADS * B_HEAD_DIM
    dn = (((1,), (1,)), ((), ()))
    pairs = [(s, kh) for s in range(SAMPLE_SEQS_PER_STEP) for kh in range(B_KV_HEADS)]
    scores = []
    for s, kh in pairs:
        ksl = slice(kh * B_HEAD_DIM, (kh + 1) * B_HEAD_DIM)
        q = (q_ref[s, kh] * scale).astype(BF16)
        s_c = lax.dot_general(q, kc_ref[s, :, ksl].astype(BF16), dn, preferred_element_type=F32)
        s_n = lax.dot_general(q, kvn_ref[s, :, ksl].astype(BF16), dn, preferred_element_type=F32)
        scores.append((s_c, s_n))
    probs = []
    for (s, kh), (s_c, s_n) in zip(pairs, scores):
        s_c = jnp.where(mask_cache, s_c, NEG_INF)
        s_n = jnp.where(mask_new, s_n, NEG_INF)
        sink = sink_ref[kh]
        m = jnp.maximum(jnp.maximum(jnp.max(s_c, axis=-1, keepdims=True),
                                    jnp.max(s_n, axis=-1, keepdims=True)), sink)
        p_c = jnp.exp(s_c - m)
        p_n = jnp.exp(s_n - m)
        denom = (jnp.sum(p_c, axis=-1, keepdims=True) + jnp.sum(p_n, axis=-1, keepdims=True)
                 + jnp.exp(sink - m))
        probs.append((p_c.astype(BF16), p_n.astype(BF16), 1.0 / denom))
    for (s, kh), (p_c, p_n, inv) in zip(pairs, probs):
        ksl = slice(kh * B_HEAD_DIM, (kh + 1) * B_HEAD_DIM)
        vsl = slice(kv_w + kh * B_HEAD_DIM, kv_w + (kh + 1) * B_HEAD_DIM)
        pv = (jnp.dot(p_c, vc_ref[s, :, ksl].astype(BF16), preferred_element_type=F32)
              + jnp.dot(p_n, kvn_ref[s, :, vsl].astype(BF16), preferred_element_type=F32))
        o_ref[s, kh] = (pv * inv).astype(o_ref.dtype)


def _swa_sample(q, cache_k, cache_v, kv_new, sink_rows, *, t_new):
    n_seq = q.shape[0]
    rows = B_GROUP * t_new
    sb = SAMPLE_SEQS_PER_STEP
    kv_w = B_KV_HEADS * B_HEAD_DIM
    return pl.pallas_call(
        functools.partial(_swa_sample_body, t_new=t_new),
        grid=(n_seq // sb,),
        in_specs=[
            pl.BlockSpec((B_KV_HEADS, rows, 1), lambda b: (0, 0, 0)),
            pl.BlockSpec((sb, B_KV_HEADS, rows, B_HEAD_DIM), lambda b: (b, 0, 0, 0)),
            pl.BlockSpec((sb, WINDOW, kv_w), lambda b: (b, 0, 0)),
            pl.BlockSpec((sb, WINDOW, kv_w), lambda b: (b, 0, 0)),
            pl.BlockSpec((sb, t_new, 2 * kv_w), lambda b: (b, 0, 0)),
        ],
        out_specs=pl.BlockSpec((sb, B_KV_HEADS, rows, B_HEAD_DIM), lambda b: (b, 0, 0, 0)),
        out_shape=jax.ShapeDtypeStruct((n_seq, B_KV_HEADS, rows, B_HEAD_DIM), BF16),
        compiler_params=_cparams("parallel"),
        name="swa_sample",
    )(sink_rows, q, cache_k, cache_v, kv_new)


def kernel(x_prompt, x_sample, state_hgrn, cache_k_window, cache_v_window, norm_g, ffn_w_in, ffn_w_out,
           a_w_in, a_lower_bounds, a_gnorm, a_w_out, kv_norm, kv_w, kv_b, b_w_q, b_b_q, b_sinks, b_w_o, b_b_o):
    n_pb, p_len, _ = x_prompt.shape
    n_sb, s_len, _ = x_sample.shape
    n_p = n_pb * p_len
    n_s = n_sb * s_len
    kv_cols = B_KV_HEADS * B_HEAD_DIM

    x = (x_prompt.reshape(n_p, D_MODEL), x_sample.reshape(n_s, D_MODEL))

    ffn_w = (ffn_w_in[0, 0].astype(BF16), ffn_w_out[0, 0].astype(BF16))
    a_w_in_b = a_w_in.astype(BF16)
    a_w_out_b = a_w_out.astype(BF16)
    kv_w_b = kv_w.astype(BF16)
    b_w_q_b = b_w_q.astype(BF16)
    b_w_o_b = b_w_o.astype(BF16)
    zero_bias = jnp.zeros((1, D_MODEL), F32)

    lb_soft = jax.nn.softmax(a_lower_bounds.astype(F32), axis=0)
    lbs = jnp.cumsum(lb_soft, axis=0) - lb_soft[0]
    pos = lbs > 0
    log_lb = jnp.where(pos, jnp.log(jnp.where(pos, lbs, 1.0)), NEG_INF)
    log1m_lb = jnp.log1p(-lbs)
    onem_lb = 1.0 - lbs

    cache_k = cache_k_window.reshape(n_sb, WINDOW, kv_cols)
    cache_v = cache_v_window.reshape(n_sb, WINDOW, kv_cols)

    def g_row(l, j):
        return norm_g[l, j].reshape(1, D_MODEL)

    s_prompt = s_sample = None
    kv = None
    for l in range(DEPTH):
        x, ffn_w = _ffn(x, g_row(l, 0), g_row(l, 1), *ffn_w, cast_next=(ffn_w_in, ffn_w_out, l, 1))
        if l < N_A_LAYERS:
            z, qig = _hgrn_in(x, g_row(l, 2), a_w_in_b, l)
            lb_terms = (log_lb[l].reshape(1, D_MODEL), log1m_lb[l].reshape(1, D_MODEL),
                        onem_lb[l].reshape(1, D_MODEL))
            gn = a_gnorm[l].reshape(1, A_DV)
            p_chunk = math.gcd(p_len, PROMPT_CHUNK)
            o_p, s_prompt = _hgrn_scan(z, qig, lb_terms, gn, None, s_prompt, layer=l, row0=0, n_seq=n_pb,
                                       seq_len=p_len, chunk=p_chunk,
                                       subs=math.gcd(p_len // p_chunk, PROMPT_CHUNKS_PER_STEP), seqs=1,
                                       name="hgrn_prompt")
            o_s, s_sample = _hgrn_scan(z, qig, lb_terms, gn, state_hgrn, s_sample, layer=l, row0=n_p,
                                       n_seq=n_sb, seq_len=s_len, chunk=s_len, subs=1,
                                       seqs=SAMPLE_SEQS_PER_SCAN_STEP, name="hgrn_sample")
            x = _post_mm(o_p, o_s, a_w_out_b[l], zero_bias, g_row(l, 3), x, "hgrn_out")
        else:
            j = l - N_A_LAYERS
            q = _pre_mm(x, g_row(l, 2), b_w_q_b[j], b_b_q[j].reshape(1, D_MODEL), "swa_q")
            o_p = _swa_prompt(q, kv, b_sinks[j], n_seq=n_pb, seq_len=p_len)
            q_s = q[n_p:].reshape(n_sb, s_len, B_KV_HEADS, B_GROUP, B_HEAD_DIM)
            q_s = q_s.transpose(0, 2, 3, 1, 4).reshape(n_sb, B_KV_HEADS, B_GROUP * s_len, B_HEAD_DIM)
            sink_rows = jnp.repeat(b_sinks[j].reshape(B_KV_HEADS, B_GROUP), s_len, axis=1)[..., None]
            o_s = _swa_sample(q_s, cache_k, cache_v, kv[n_p:].reshape(n_sb, s_len, 2 * kv_cols),
                              sink_rows, t_new=s_len)
            o_s = o_s.reshape(n_sb, B_KV_HEADS, B_GROUP, s_len, B_HEAD_DIM)
            o_s = o_s.transpose(0, 3, 1, 2, 4).reshape(n_s, D_MODEL)
            x = _post_mm(o_p, o_s, b_w_o_b[j], b_b_o[j].reshape(1, D_MODEL), g_row(l, 3), x, "swa_out")
        last = l == DEPTH - 1
        x, ffn_w = _ffn(x, g_row(l, 4), g_row(l, 5), *ffn_w, split_rows=(n_p, n_s) if last else None,
                        cast_next=None if last else (ffn_w_in, ffn_w_out, l + 1, 0))
        if l == N_A_LAYERS - 1:
            kv = _pre_mm(x, kv_norm.reshape(1, D_MODEL), kv_w_b, kv_b.reshape(1, 2 * kv_cols), "shared_kv")

    y_prompt = x[0].reshape(n_pb, p_len, D_MODEL)
    y_sample = x[1].reshape(n_sb, s_len, D_MODEL)

    kv_p = jnp.stack([kv[(b + 1) * p_len - WINDOW:(b + 1) * p_len] for b in range(n_pb)])
    kv_p = kv_p.reshape(n_pb, WINDOW, 2, B_KV_HEADS, B_HEAD_DIM)
    kv_s = kv[n_p:].reshape(n_sb, s_len, 2, B_KV_HEADS, B_HEAD_DIM)
    k_win_sample = jnp.concatenate([cache_k_window, kv_s[:, :, 0]], axis=1)[:, -WINDOW:]
    v_win_sample = jnp.concatenate([cache_v_window, kv_s[:, :, 1]], axis=1)[:, -WINDOW:]
    return (y_prompt, y_sample, s_prompt, s_sample,
            kv_p[:, :, 0], kv_p[:, :, 1], k_win_sample, v_win_sample)
```

```python
import functools
import math

import jax
import jax.numpy as jnp
from jax import lax
from jax.experimental import pallas as pl
from jax.experimental.pallas import tpu as pltpu

F32 = jnp.float32
BF16 = jnp.bfloat16

D_MODEL = 1024
DEPTH = 4
N_A_LAYERS = 2
A_HEADS = 8
A_DK = 128
A_DV = 128
B_HEAD_DIM = 64
B_HEADS = 16
B_KV_HEADS = 2
B_GROUP = 8
WINDOW = 128
D_FF = 2816
EPS = 1e-6
NEG_INF = -1e30

VMEM_LIMIT_BYTES = 56 * 1024 * 1024

ROW_TILE = 1024
FFN_TILE = 256
FFN_ROW_SPLIT = 2
WEIGHT_CAST_BLOCKS = 16
PROMPT_CHUNK = 64
PROMPT_CHUNKS_PER_STEP = 8
OUT_PROJ_ROWS = 256
SWA_BLOCKS_PER_STEP = 4
SAMPLE_SEQS_PER_STEP = 8
SAMPLE_SEQS_PER_SCAN_STEP = 4


def _rms(x, g):
    return x * lax.rsqrt(jnp.mean(x * x, axis=-1, keepdims=True) + EPS) * g


def _sigmoid(x):
    return 1.0 / (1.0 + jnp.exp(-x))


def _cparams(*sem):
    return pltpu.CompilerParams(dimension_semantics=sem, vmem_limit_bytes=VMEM_LIMIT_BYTES)


def _lead_tail_specs(n_lead_tiles, cols, row_tile=ROW_TILE):
    return [pl.BlockSpec((row_tile, cols), lambda i: (jnp.minimum(i, n_lead_tiles - 1), 0)),
            pl.BlockSpec((row_tile, cols), lambda i: (jnp.maximum(i - n_lead_tiles, 0), 0))]


def _ffn_body(*refs, n_src, n_dst, n_lead_tiles, cast_next):
    refs = list(refs)
    x_refs = [refs.pop(0) for _ in range(n_src)]
    gpre_ref, gpost_ref, win_ref, wout_ref = [refs.pop(0) for _ in range(4)]
    next_f32 = [refs.pop(0) for _ in range(2)] if cast_next else []
    out_refs = [refs.pop(0) for _ in range(n_dst)]
    next_bf16 = [refs.pop(0) for _ in range(2)] if cast_next else []
    h_ref = refs.pop(0)
    i = pl.program_id(0)
    nf = D_FF // FFN_TILE
    rows = ROW_TILE // FFN_ROW_SPLIT

    for src, dst in zip(next_f32, next_bf16):
        dst[...] = src[...].astype(BF16)

    if n_src == 2:
        x_ref = refs.pop(0)

        @pl.when(i < n_lead_tiles)
        def _():
            x_ref[...] = x_refs[0][...]

        @pl.when(i >= n_lead_tiles)
        def _():
            x_ref[...] = x_refs[1][...]
    else:
        x_ref = x_refs[0]
    out_ref = out_refs[-1]

    for f in range(nf):
        c0, c1 = f * FFN_TILE, (f + 1) * FFN_TILE
        wg = win_ref[:, c0:c1]
        wu = win_ref[:, D_FF + c0:D_FF + c1]
        wo = wout_ref[c0:c1, :]
        for r in range(FFN_ROW_SPLIT):
            rs = slice(r * rows, (r + 1) * rows)
            if f == 0:
                h = _rms(x_ref[rs, :], gpre_ref[...]).astype(BF16)
                h_ref[rs, :] = h
            else:
                h = h_ref[rs, :]
            gate = jnp.dot(h, wg, preferred_element_type=F32)
            up = jnp.dot(h, wu, preferred_element_type=F32)
            act = (gate * _sigmoid(gate) * up).astype(BF16)
            part = jnp.dot(act, wo, preferred_element_type=F32)
            if f == 0:
                out_ref[rs, :] = part
            elif f == nf - 1:
                out_ref[rs, :] = x_ref[rs, :] + 0.5 * _rms(out_ref[rs, :] + part, gpost_ref[...])
            else:
                out_ref[rs, :] += part

    if n_dst == 2:
        @pl.when(i < n_lead_tiles)
        def _():
            out_refs[0][...] = out_ref[...]


def _ffn(xs, g_pre, g_post, w_in, w_out, split_rows=None, cast_next=None):
    xs = xs if isinstance(xs, (tuple, list)) else (xs,)
    n = sum(x.shape[0] for x in xs)
    n_tiles = n // ROW_TILE
    assert D_FF // FFN_TILE >= 2
    n_lead = xs[0].shape[0] if len(xs) == 2 else (split_rows[0] if split_rows else n)
    assert split_rows is None or len(xs) == 1 or split_rows[0] == xs[0].shape[0]
    n_lead_tiles = n_lead // ROW_TILE
    resident = pl.Buffered(1)
    row_spec = pl.BlockSpec((ROW_TILE, D_MODEL), lambda i: (i, 0))
    in_specs = _lead_tail_specs(n_lead_tiles, D_MODEL) if len(xs) == 2 else [row_spec]
    in_specs += [
        pl.BlockSpec((1, D_MODEL), lambda i: (0, 0)),
        pl.BlockSpec((1, D_MODEL), lambda i: (0, 0)),
        pl.BlockSpec((D_MODEL, 2 * D_FF), lambda i: (0, 0), pipeline_mode=resident),
        pl.BlockSpec((D_FF, D_MODEL), lambda i: (0, 0), pipeline_mode=resident),
    ]
    args = [*xs, g_pre, g_post, w_in, w_out]
    scratch = [pltpu.VMEM((ROW_TILE, D_MODEL), BF16)]
    if len(xs) == 2:
        scratch.append(pltpu.VMEM((ROW_TILE, D_MODEL), F32))
    if split_rows:
        out_specs = _lead_tail_specs(n_lead_tiles, D_MODEL)
        out_shape = [jax.ShapeDtypeStruct((r, D_MODEL), F32) for r in split_rows]
    else:
        out_specs = [row_spec]
        out_shape = [jax.ShapeDtypeStruct((n, D_MODEL), F32)]
    n_dst = len(out_specs)
    if cast_next:
        wn_in, wn_out, layer, which = cast_next
        assert n_tiles >= WEIGHT_CAST_BLOCKS
        for w in (wn_in, wn_out):
            rows, cols = w.shape[2] // WEIGHT_CAST_BLOCKS, w.shape[3]
            assert rows * WEIGHT_CAST_BLOCKS == w.shape[2]
            in_specs.append(pl.BlockSpec(
                (None, None, rows, cols),
                lambda i: (layer, which, jnp.minimum(i, WEIGHT_CAST_BLOCKS - 1), 0)))
            out_specs.append(pl.BlockSpec((rows, cols), lambda i: (jnp.minimum(i, WEIGHT_CAST_BLOCKS - 1), 0)))
            out_shape.append(jax.ShapeDtypeStruct(w.shape[2:], BF16))
            args.append(w)
    outs = pl.pallas_call(
        functools.partial(_ffn_body, n_src=len(xs), n_dst=n_dst, n_lead_tiles=n_lead_tiles,
                          cast_next=bool(cast_next)),
        grid=(n_tiles,),
        in_specs=in_specs,
        out_specs=out_specs,
        out_shape=out_shape,
        scratch_shapes=scratch,
        compiler_params=_cparams("arbitrary"),
        name="ffn",
    )(*args)
    y = outs[:n_dst] if split_rows else outs[0]
    return y, (tuple(outs[n_dst:]) if cast_next else None)


def _pre_mm_body(x_ref, g_ref, w_ref, b_ref, out_ref, h_ref):
    @pl.when(pl.program_id(1) == 0)
    def _():
        h_ref[...] = _rms(x_ref[...], g_ref[...]).astype(BF16)

    out_ref[...] = jnp.dot(h_ref[...], w_ref[...], preferred_element_type=F32) + b_ref[...]


def _pre_mm(x, g, w, b, name):
    n = x.shape[0]
    n_out = w.shape[1]
    tn = min(n_out, 1024)
    return pl.pallas_call(
        _pre_mm_body,
        grid=(n // ROW_TILE, n_out // tn),
        in_specs=[
            pl.BlockSpec((ROW_TILE, D_MODEL), lambda i, j: (i, 0)),
            pl.BlockSpec((1, D_MODEL), lambda i, j: (0, 0)),
            pl.BlockSpec((D_MODEL, tn), lambda i, j: (0, j)),
            pl.BlockSpec((1, tn), lambda i, j: (0, j)),
        ],
        out_specs=pl.BlockSpec((ROW_TILE, tn), lambda i, j: (i, j)),
        out_shape=jax.ShapeDtypeStruct((n, n_out), F32),
        scratch_shapes=[pltpu.VMEM((ROW_TILE, D_MODEL), BF16)],
        compiler_params=_cparams("parallel", "arbitrary"),
        name=name,
    )(x, g, w, b)


def _hgrn_in_body(x_ref, g_ref, w_ref, z_ref, qig_ref):
    rows = ROW_TILE // 2
    d = D_MODEL
    for r in range(2):
        rs = slice(r * rows, (r + 1) * rows)
        h = _rms(x_ref[rs, :], g_ref[...]).astype(BF16)
        z_ref[rs, :] = jnp.dot(h, w_ref[:, d:2 * d], preferred_element_type=F32)
        for j, c0 in enumerate((0, 2 * d, 3 * d)):
            qig_ref[rs, j * d:(j + 1) * d] = jnp.dot(h, w_ref[:, c0:c0 + d],
                                                     preferred_element_type=F32).astype(BF16)


def _hgrn_in(x, g, w, layer):
    n = x.shape[0]
    return pl.pallas_call(
        _hgrn_in_body,
        grid=(n // ROW_TILE,),
        in_specs=[
            pl.BlockSpec((ROW_TILE, D_MODEL), lambda i: (i, 0)),
            pl.BlockSpec((1, D_MODEL), lambda i: (0, 0)),
            pl.BlockSpec((None, D_MODEL, 4 * D_MODEL), lambda i: (layer, 0, 0), pipeline_mode=pl.Buffered(1)),
        ],
        out_specs=[
            pl.BlockSpec((ROW_TILE, D_MODEL), lambda i: (i, 0)),
            pl.BlockSpec((ROW_TILE, 3 * D_MODEL), lambda i: (i, 0)),
        ],
        out_shape=[
            jax.ShapeDtypeStruct((n, D_MODEL), F32),
            jax.ShapeDtypeStruct((n, 3 * D_MODEL), BF16),
        ],
        compiler_params=_cparams("parallel"),
        name="hgrn_in",
    )(x, g, w)


def _residual_update(o, w_ref, b_ref, g_ref, x):
    return x + _rms(jnp.dot(o, w_ref[...], preferred_element_type=F32) + b_ref[...], g_ref[...])


def _post_mm_body(o_ref, w_ref, b_ref, g_ref, x_ref, out_ref):
    out_ref[...] = _residual_update(o_ref[...], w_ref, b_ref, g_ref, x_ref[...])


def _post_mm_rows(o, w, b, g, x, row0, name):
    tile0 = row0 // ROW_TILE
    assert tile0 * ROW_TILE == row0
    x_spec = pl.BlockSpec((ROW_TILE, D_MODEL), lambda i: (tile0 + i, 0))
    return pl.pallas_call(
        _post_mm_body,
        grid=(o.shape[0] // ROW_TILE,),
        in_specs=[
            pl.BlockSpec((ROW_TILE, D_MODEL), lambda i: (i, 0)),
            pl.BlockSpec((D_MODEL, D_MODEL), lambda i: (0, 0)),
            pl.BlockSpec((1, D_MODEL), lambda i: (0, 0)),
            pl.BlockSpec((1, D_MODEL), lambda i: (0, 0)),
            x_spec,
        ],
        out_specs=x_spec,
        out_shape=jax.ShapeDtypeStruct(x.shape, x.dtype),
        input_output_aliases={4: 0},
        compiler_params=_cparams("parallel"),
        name=name,
    )(o, w, b, g, x)


def _split3(a):
    a1 = a.astype(BF16)
    r1 = a - a1.astype(F32)
    a2 = r1.astype(BF16)
    a3 = (r1 - a2.astype(F32)).astype(BF16)
    return a1, a2, a3


def _hgrn_body(*refs, chunk, subs, seqs, layer, has_init, has_prev_out, fuse_out):
    refs = list(refs)
    q_ref, z_ref, i_ref, g_ref, loglb_ref, log1mlb_ref, onemlb_ref, gn_ref = refs[:8]
    del refs[:8]
    s0_ref = refs.pop(0) if has_init else None
    if has_prev_out:
        refs.pop(0)
    if fuse_out:
        x_ref, wout_ref, bout_ref, gpost_ref = [refs.pop(0) for _ in range(4)]
    o_ref, sout_ref, st_ref = refs
    pending = []
    c = pl.program_id(1)
    n_chunks = pl.num_programs(1)

    @pl.when(c == 0)
    def _():
        if has_init:
            for s in range(seqs):
                for h in range(A_HEADS):
                    st_ref[s * A_HEADS + h] = s0_ref[s, h].T
        else:
            st_ref[...] = jnp.zeros_like(st_ref)

    row = lax.broadcasted_iota(jnp.int32, (chunk, chunk), 0)
    col = lax.broadcasted_iota(jnp.int32, (chunk, chunk), 1)
    causal = col <= row
    tril = causal.astype(BF16)
    mid = chunk // 2
    gn = gn_ref[...]
    log_lb = loglb_ref[...]
    log1m_lb = log1mlb_ref[...]
    onem_lb = onemlb_ref[...]
    dn_t = (((1,), (1,)), ((), ()))
    dn_0 = (((0,), (0,)), ((), ()))

    for s, u in [(s, u) for s in range(seqs) for u in range(subs)]:
        r0 = (s * subs + u) * chunk
        rs = slice(r0, r0 + chunk)
        q = q_ref[rs, :].astype(F32)
        z = z_ref[rs, :]
        g = g_ref[rs, :].astype(F32)
        vb = i_ref[rs, :]

        e = jnp.exp(-jnp.abs(z))
        r = 1.0 / (1.0 + e)
        log_sig = jnp.minimum(z, 0.0) - jnp.log(1.0 + e)
        b_ = log1m_lb + log_sig
        log_f = jnp.maximum(log_lb, b_) + jnp.log(1.0 + jnp.exp(-jnp.abs(log_lb - b_)))
        k = onem_lb * jnp.where(z >= 0.0, e * r, r)

        f1, f2, f3 = _split3(log_f)
        cum = (jnp.dot(tril, f1, preferred_element_type=F32)
               + jnp.dot(tril, f2, preferred_element_type=F32)
               + jnp.dot(tril, f3, preferred_element_type=F32))
        last = cum[chunk - 1:chunk, :]
        ref_row = cum[mid:mid + 1, :]

        dec = jnp.exp(cum - ref_row)
        q_dec = q * dec
        k_dec = k * (1.0 / dec)
        qs = q_dec.astype(BF16)
        ks = k_dec.astype(BF16)
        qe = (q_dec * jnp.exp(ref_row)).astype(BF16)
        ke = (k_dec * jnp.exp(last - ref_row)).astype(BF16)
        e_last = jnp.exp(last)

        lanes = [slice(h * A_DK, (h + 1) * A_DK) for h in range(A_HEADS)]
        scores = [lax.dot_general(qs[:, sl], ks[:, sl], dn_t, preferred_element_type=F32) for sl in lanes]
        sts = [st_ref[s * A_HEADS + h] for h in range(A_HEADS)]
        inter = [lax.dot_general(qe[:, sl], st.astype(BF16), dn_t, preferred_element_type=F32)
                 for sl, st in zip(lanes, sts)]
        kv = [lax.dot_general(vb[:, sl], ke[:, sl], dn_0, preferred_element_type=F32) for sl in lanes]
        probs = [jnp.where(causal, sc, 0.0).astype(BF16) for sc in scores]
        outs = [jnp.dot(p, vb[:, sl], preferred_element_type=F32) + it
                for p, sl, it in zip(probs, lanes, inter)]
        for h, sl in enumerate(lanes):
            st_ref[s * A_HEADS + h] = e_last[:, sl] * sts[h] + kv[h]
        o = jnp.concatenate([_rms(o_h, gn) for o_h in outs], axis=-1)
        gated = (o * (g * _sigmoid(g))).astype(BF16)
        if not fuse_out:
            o_ref[rs, :] = gated
            continue
        pending.append(gated)
        done = r0 + chunk
        if done % OUT_PROJ_ROWS == 0:
            ro = slice(done - OUT_PROJ_ROWS, done)
            o_ref[ro, :] = _residual_update(jnp.concatenate(pending, axis=0), wout_ref, bout_ref, gpost_ref,
                                            x_ref[ro, :])
            pending.clear()

    @pl.when(c == n_chunks - 1)
    def _():
        own = sout_ref if has_prev_out else sout_ref.at[layer]
        for s in range(seqs):
            for h in range(A_HEADS):
                own[s, h] = st_ref[s * A_HEADS + h].T
        if not has_prev_out:
            for other in range(N_A_LAYERS):
                if other != layer:
                    sout_ref[other] = jnp.zeros(sout_ref.shape[1:], sout_ref.dtype)


def _hgrn_scan(z, qig, lb_terms, gn, s0, s_prev, *, layer, row0, n_seq, seq_len, chunk, subs, seqs, name,
               out_proj=None):
    n_chunks = seq_len // (subs * chunk)
    blk = seqs * subs * chunk
    assert n_chunks == 1 or seqs == 1
    blk0 = row0 // blk
    has_init = s0 is not None
    has_prev = s_prev is not None

    def rows_at(j):
        return pl.BlockSpec((blk, D_MODEL), lambda b, c: (blk0 + b * n_chunks + c, j))

    in_specs = [rows_at(0), rows_at(0), rows_at(1), rows_at(2)]
    in_specs += [pl.BlockSpec((1, D_MODEL), lambda b, c: (0, 0))] * 3
    in_specs += [pl.BlockSpec((1, A_DV), lambda b, c: (0, 0))]
    args = [qig, z, qig, qig, *lb_terms, gn]
    if has_init:
        in_specs.append(pl.BlockSpec((None, seqs, A_HEADS, A_DK, A_DV), lambda b, c: (layer, b, 0, 0, 0)))
        args.append(s0)
    aliases = {}
    if has_prev:
        in_specs.append(pl.BlockSpec(memory_space=pl.ANY))
        aliases = {len(args): 1}
        args.append(s_prev)
    if has_prev:
        state_spec = pl.BlockSpec((None, seqs, A_HEADS, A_DK, A_DV), lambda b, c: (layer, b, 0, 0, 0))
    else:
        state_spec = pl.BlockSpec((N_A_LAYERS, seqs, A_HEADS, A_DK, A_DV), lambda b, c: (0, b, 0, 0, 0))
    o_spec = pl.BlockSpec((blk, D_MODEL), lambda b, c: (b * n_chunks + c, 0))
    scratch = [pltpu.VMEM((seqs * A_HEADS, A_DV, A_DK), F32)]
    if out_proj is not None:
        x, w_out, b_out, g_post = out_proj
        assert row0 == 0 and blk % OUT_PROJ_ROWS == 0
        aliases[len(args)] = 0
        in_specs += [o_spec,
                     pl.BlockSpec((D_MODEL, D_MODEL), lambda b, c: (0, 0), pipeline_mode=pl.Buffered(1)),
                     pl.BlockSpec((1, D_MODEL), lambda b, c: (0, 0)),
                     pl.BlockSpec((1, D_MODEL), lambda b, c: (0, 0))]
        args += [x, w_out, b_out, g_post]
        o_shape = jax.ShapeDtypeStruct(x.shape, x.dtype)
    else:
        o_shape = jax.ShapeDtypeStruct((n_seq * seq_len, D_MODEL), BF16)
    return pl.pallas_call(
        functools.partial(_hgrn_body, chunk=chunk, subs=subs, seqs=seqs, layer=layer, has_init=has_init,
                          has_prev_out=has_prev, fuse_out=out_proj is not None),
        grid=(n_seq // seqs, n_chunks),
        in_specs=in_specs,
        out_specs=[o_spec, state_spec],
        out_shape=[o_shape, jax.ShapeDtypeStruct((N_A_LAYERS, n_seq, A_HEADS, A_DK, A_DV), F32)],
        scratch_shapes=scratch,
        input_output_aliases=aliases,
        compiler_params=_cparams("parallel", "arbitrary"),
        name=name,
    )(*args)


def _swa_prompt_body(sink_ref, q_ref, kvp_ref, kvc_ref, x_ref, wo_ref, bo_ref, gpost_ref, xnew_ref):
    i = pl.program_id(1)
    nq = q_ref.shape[0] // WINDOW
    key = lax.broadcasted_iota(jnp.int32, (2 * WINDOW, WINDOW), 0)
    qry = lax.broadcasted_iota(jnp.int32, (2 * WINDOW, WINDOW), 1)
    band = ((key < WINDOW) & (key >= qry)) | ((key >= WINDOW) & (key - WINDOW <= qry))
    first_band = band & ((key >= WINDOW) | (i > 0))
    scale = B_HEAD_DIM ** -0.5
    kv_w = B_KV_HEADS * B_HEAD_DIM
    dn_t = (((1,), (1,)), ((), ()))
    dn_0 = (((0,), (0,)), ((), ()))
    pairs = [(j, kh) for j in range(nq) for kh in range(B_KV_HEADS)]

    kv_rows = jnp.concatenate([kvp_ref[...], kvc_ref[...]], axis=0).astype(BF16)
    scores = []
    for j, kh in pairs:
        k = kv_rows[j * WINDOW:(j + 2) * WINDOW, kh * B_HEAD_DIM:(kh + 1) * B_HEAD_DIM]
        heads = range(kh * B_GROUP, (kh + 1) * B_GROUP)
        q = jnp.concatenate([q_ref[j * WINDOW:(j + 1) * WINDOW, h * B_HEAD_DIM:(h + 1) * B_HEAD_DIM]
                             for h in heads], axis=0)
        q = (q * scale).astype(BF16)
        scores.append(lax.dot_general(k, q, dn_t, preferred_element_type=F32))
    probs = []
    for (j, kh), s_t in zip(pairs, scores):
        mask = first_band if j == 0 else band
        p_parts, inv_parts = [], []
        for gi in range(B_GROUP):
            s = jnp.where(mask, s_t[:, gi * WINDOW:(gi + 1) * WINDOW], NEG_INF)
            sink = sink_ref[kh * B_GROUP + gi]
            m = jnp.maximum(jnp.max(s, axis=0, keepdims=True), sink)
            p = jnp.exp(s - m)
            inv_parts.append(1.0 / (jnp.sum(p, axis=0, keepdims=True) + jnp.exp(sink - m)))
            p_parts.append(p.astype(BF16))
        probs.append((jnp.concatenate(p_parts, axis=1), jnp.concatenate(inv_parts, axis=1)))
    outs = {}
    for (j, kh), (p_t, inv) in zip(pairs, probs):
        v = kv_rows[j * WINDOW:(j + 2) * WINDOW, kv_w + kh * B_HEAD_DIM:kv_w + (kh + 1) * B_HEAD_DIM]
        outs[j, kh] = lax.dot_general(v, p_t, dn_0, preferred_element_type=F32) * inv
    pending = []
    for j in range(nq):
        o_t = jnp.concatenate([outs[j, kh][:, gi * WINDOW:(gi + 1) * WINDOW]
                               for kh in range(B_KV_HEADS) for gi in range(B_GROUP)], axis=0)
        pending.append(o_t.T.astype(BF16))
        done = (j + 1) * WINDOW
        if done % OUT_PROJ_ROWS == 0:
            ro = slice(done - OUT_PROJ_ROWS, done)
            xnew_ref[ro, :] = _residual_update(jnp.concatenate(pending, axis=0), wo_ref, bo_ref, gpost_ref,
                                               x_ref[ro, :])
            pending.clear()


def _swa_prompt(q, kv, sinks, x, w_o, b_o, g_post, *, n_seq, seq_len):
    nq = math.gcd(seq_len // WINDOW, SWA_BLOCKS_PER_STEP)
    rows = nq * WINDOW
    assert rows % OUT_PROJ_ROWS == 0
    nb = seq_len // rows
    kv_cols = kv.shape[1]
    row_spec = pl.BlockSpec((rows, D_MODEL), lambda b, i: (b * nb + i, 0))
    return pl.pallas_call(
        _swa_prompt_body,
        grid=(n_seq, nb),
        in_specs=[
            pl.BlockSpec(memory_space=pltpu.SMEM),
            row_spec,
            pl.BlockSpec((WINDOW, kv_cols), lambda b, i: (jnp.maximum((b * nb + i) * nq - 1, 0), 0)),
            pl.BlockSpec((rows, kv_cols), lambda b, i: (b * nb + i, 0)),
            row_spec,
            pl.BlockSpec((D_MODEL, D_MODEL), lambda b, i: (0, 0), pipeline_mode=pl.Buffered(1)),
            pl.BlockSpec((1, D_MODEL), lambda b, i: (0, 0)),
            pl.BlockSpec((1, D_MODEL), lambda b, i: (0, 0)),
        ],
        out_specs=row_spec,
        out_shape=jax.ShapeDtypeStruct(x.shape, x.dtype),
        input_output_aliases={4: 0},
        compiler_params=_cparams("parallel", "arbitrary"),
        name="swa_prompt",
    )(sinks, q, kv, kv, x, w_o, b_o, g_post)


def _swa_sample_body(sink_ref, q_ref, kc_ref, vc_ref, kvn_ref, o_ref, *, t_new):
    rows = B_GROUP * t_new
    t_row = lax.broadcasted_iota(jnp.int32, (rows, WINDOW), 0) % t_new
    c_col = lax.broadcasted_iota(jnp.int32, (rows, WINDOW), 1)
    mask_cache = c_col >= t_row
    t_row_n = lax.broadcasted_iota(jnp.int32, (rows, t_new), 0) % t_new
    t_col_n = lax.broadcasted_iota(jnp.int32, (rows, t_new), 1)
    mask_new = t_col_n <= t_row_n
    scale = B_HEAD_DIM ** -0.5
    kv_w = B_KV_HEADS * B_HEAD_DIM
    dn = (((1,), (1,)), ((), ()))
    pairs = [(s, kh) for s in range(SAMPLE_SEQS_PER_STEP) for kh in range(B_KV_HEADS)]
    scores = []
    for s, kh in pairs:
        ksl = slice(kh * B_HEAD_DIM, (kh + 1) * B_HEAD_DIM)
        q = (q_ref[s, kh] * scale).astype(BF16)
        s_c = lax.dot_general(q, kc_ref[s, :, ksl].astype(BF16), dn, preferred_element_type=F32)
        s_n = lax.dot_general(q, kvn_ref[s, :, ksl].astype(BF16), dn, preferred_element_type=F32)
        scores.append((s_c, s_n))
    probs = []
    for (s, kh), (s_c, s_n) in zip(pairs, scores):
        s_c = jnp.where(mask_cache, s_c, NEG_INF)
        s_n = jnp.where(mask_new, s_n, NEG_INF)
        sink = sink_ref[kh]
        m = jnp.maximum(jnp.maximum(jnp.max(s_c, axis=-1, keepdims=True),
                                    jnp.max(s_n, axis=-1, keepdims=True)), sink)
        p_c = jnp.exp(s_c - m)
        p_n = jnp.exp(s_n - m)
        denom = (jnp.sum(p_c, axis=-1, keepdims=True) + jnp.sum(p_n, axis=-1, keepdims=True)
                 + jnp.exp(sink - m))
        probs.append((p_c.astype(BF16), p_n.astype(BF16), 1.0 / denom))
    for (s, kh), (p_c, p_n, inv) in zip(pairs, probs):
        ksl = slice(kh * B_HEAD_DIM, (kh + 1) * B_HEAD_DIM)
        vsl = slice(kv_w + kh * B_HEAD_DIM, kv_w + (kh + 1) * B_HEAD_DIM)
        pv = (jnp.dot(p_c, vc_ref[s, :, ksl].astype(BF16), preferred_element_type=F32)
              + jnp.dot(p_n, kvn_ref[s, :, vsl].astype(BF16), preferred_element_type=F32))
        o_ref[s, kh] = (pv * inv).astype(o_ref.dtype)


def _swa_sample(q, cache_k, cache_v, kv_new, sink_rows, *, t_new):
    n_seq = q.shape[0]
    rows = B_GROUP * t_new
    sb = SAMPLE_SEQS_PER_STEP
    kv_w = B_KV_HEADS * B_HEAD_DIM
    return pl.pallas_call(
        functools.partial(_swa_sample_body, t_new=t_new),
        grid=(n_seq // sb,),
        in_specs=[
            pl.BlockSpec((B_KV_HEADS, rows, 1), lambda b: (0, 0, 0)),
            pl.BlockSpec((sb, B_KV_HEADS, rows, B_HEAD_DIM), lambda b: (b, 0, 0, 0)),
            pl.BlockSpec((sb, WINDOW, kv_w), lambda b: (b, 0, 0)),
            pl.BlockSpec((sb, WINDOW, kv_w), lambda b: (b, 0, 0)),
            pl.BlockSpec((sb, t_new, 2 * kv_w), lambda b: (b, 0, 0)),
        ],
        out_specs=pl.BlockSpec((sb, B_KV_HEADS, rows, B_HEAD_DIM), lambda b: (b, 0, 0, 0)),
        out_shape=jax.ShapeDtypeStruct((n_seq, B_KV_HEADS, rows, B_HEAD_DIM), BF16),
        compiler_params=_cparams("parallel"),
        name="swa_sample",
    )(sink_rows, q, cache_k, cache_v, kv_new)


def kernel(x_prompt, x_sample, state_hgrn, cache_k_window, cache_v_window, norm_g, ffn_w_in, ffn_w_out,
           a_w_in, a_lower_bounds, a_gnorm, a_w_out, kv_norm, kv_w, kv_b, b_w_q, b_b_q, b_sinks, b_w_o, b_b_o):
    n_pb, p_len, _ = x_prompt.shape
    n_sb, s_len, _ = x_sample.shape
    n_p = n_pb * p_len
    n_s = n_sb * s_len
    kv_cols = B_KV_HEADS * B_HEAD_DIM

    x = (x_prompt.reshape(n_p, D_MODEL), x_sample.reshape(n_s, D_MODEL))

    ffn_w = (ffn_w_in[0, 0].astype(BF16), ffn_w_out[0, 0].astype(BF16))
    a_w_in_b = a_w_in.astype(BF16)
    a_w_out_b = a_w_out.astype(BF16)
    kv_w_b = kv_w.astype(BF16)
    b_w_q_b = b_w_q.astype(BF16)
    b_w_o_b = b_w_o.astype(BF16)
    zero_bias = jnp.zeros((1, D_MODEL), F32)

    lb_soft = jax.nn.softmax(a_lower_bounds.astype(F32), axis=0)
    lbs = jnp.cumsum(lb_soft, axis=0) - lb_soft[0]
    pos = lbs > 0
    log_lb = jnp.where(pos, jnp.log(jnp.where(pos, lbs, 1.0)), NEG_INF)
    log1m_lb = jnp.log1p(-lbs)
    onem_lb = 1.0 - lbs

    cache_k = cache_k_window.reshape(n_sb, WINDOW, kv_cols)
    cache_v = cache_v_window.reshape(n_sb, WINDOW, kv_cols)

    def g_row(l, j):
        return norm_g[l, j].reshape(1, D_MODEL)

    s_prompt = s_sample = None
    kv = None
    for l in range(DEPTH):
        x, ffn_w = _ffn(x, g_row(l, 0), g_row(l, 1), *ffn_w, cast_next=(ffn_w_in, ffn_w_out, l, 1))
        if l < N_A_LAYERS:
            z, qig = _hgrn_in(x, g_row(l, 2), a_w_in_b, l)
            lb_terms = (log_lb[l].reshape(1, D_MODEL), log1m_lb[l].reshape(1, D_MODEL),
                        onem_lb[l].reshape(1, D_MODEL))
            gn = a_gnorm[l].reshape(1, A_DV)
            p_chunk = math.gcd(p_len, PROMPT_CHUNK)
            out_proj = (a_w_out_b[l], zero_bias, g_row(l, 3))
            x, s_prompt = _hgrn_scan(z, qig, lb_terms, gn, None, s_prompt, layer=l, row0=0, n_seq=n_pb,
                                     seq_len=p_len, chunk=p_chunk,
                                     subs=math.gcd(p_len // p_chunk, PROMPT_CHUNKS_PER_STEP), seqs=1,
                                     name="hgrn_prompt", out_proj=(x, *out_proj))
            o_s, s_sample = _hgrn_scan(z, qig, lb_terms, gn, state_hgrn, s_sample, layer=l, row0=n_p,
                                       n_seq=n_sb, seq_len=s_len, chunk=s_len, subs=1,
                                       seqs=SAMPLE_SEQS_PER_SCAN_STEP, name="hgrn_sample")
            x = _post_mm_rows(o_s, *out_proj, x, n_p, "hgrn_out")
        else:
            j = l - N_A_LAYERS
            out_proj = (b_w_o_b[j], b_b_o[j].reshape(1, D_MODEL), g_row(l, 3))
            q = _pre_mm(x, g_row(l, 2), b_w_q_b[j], b_b_q[j].reshape(1, D_MODEL), "swa_q")
            x = _swa_prompt(q, kv, b_sinks[j], x, *out_proj, n_seq=n_pb, seq_len=p_len)
            q_s = q[n_p:].reshape(n_sb, s_len, B_KV_HEADS, B_GROUP, B_HEAD_DIM)
            q_s = q_s.transpose(0, 2, 3, 1, 4).reshape(n_sb, B_KV_HEADS, B_GROUP * s_len, B_HEAD_DIM)
            sink_rows = jnp.repeat(b_sinks[j].reshape(B_KV_HEADS, B_GROUP), s_len, axis=1)[..., None]
            o_s = _swa_sample(q_s, cache_k, cache_v, kv[n_p:].reshape(n_sb, s_len, 2 * kv_cols),
                              sink_rows, t_new=s_len)
            o_s = o_s.reshape(n_sb, B_KV_HEADS, B_GROUP, s_len, B_HEAD_DIM)
            o_s = o_s.transpose(0, 3, 1, 2, 4).reshape(n_s, D_MODEL)
            x = _post_mm_rows(o_s, *out_proj, x, n_p, "swa_out")
        last = l == DEPTH - 1
        x, ffn_w = _ffn(x, g_row(l, 4), g_row(l, 5), *ffn_w, split_rows=(n_p, n_s) if last else None,
                        cast_next=None if last else (ffn_w_in, ffn_w_out, l + 1, 0))
        if l == N_A_LAYERS - 1:
            kv = _pre_mm(x, kv_norm.reshape(1, D_MODEL), kv_w_b, kv_b.reshape(1, 2 * kv_cols), "shared_kv")

    y_prompt = x[0].reshape(n_pb, p_len, D_MODEL)
    y_sample = x[1].reshape(n_sb, s_len, D_MODEL)

    kv_p = jnp.stack([kv[(b + 1) * p_len - WINDOW:(b + 1) * p_len] for b in range(n_pb)])
    kv_p = kv_p.reshape(n_pb, WINDOW, 2, B_KV_HEADS, B_HEAD_DIM)
    kv_s = kv[n_p:].reshape(n_sb, s_len, 2, B_KV_HEADS, B_HEAD_DIM)
    k_win_sample = jnp.concatenate([cache_k_window, kv_s[:, :, 0]], axis=1)[:, -WINDOW:]
    v_win_sample = jnp.concatenate([cache_v_window, kv_s[:, :, 1]], axis=1)[:, -WINDOW:]
    return (y_prompt, y_sample, s_prompt, s_sample,
            kv_p[:, :, 0], kv_p[:, :, 1], k_win_sample, v_win_sample)
```

```python
import functools
import math

import jax
import jax.numpy as jnp
from jax import lax
from jax.experimental import pallas as pl
from jax.experimental.pallas import tpu as pltpu

F32 = jnp.float32
BF16 = jnp.bfloat16

D_MODEL = 1024
DEPTH = 4
N_A_LAYERS = 2
A_HEADS = 8
A_DK = 128
A_DV = 128
B_HEAD_DIM = 64
B_HEADS = 16
B_KV_HEADS = 2
B_GROUP = 8
WINDOW = 128
D_FF = 2816
EPS = 1e-6
NEG_INF = -1e30

VMEM_LIMIT_BYTES = 56 * 1024 * 1024

ROW_TILE = 1024
FFN_TILE = 256
FFN_ROW_SPLIT = 2
WEIGHT_CAST_BLOCKS = 16
PROMPT_CHUNK = 64
PROMPT_CHUNKS_PER_STEP = 8
SCAN_LOOKAHEAD = 1
OUT_PROJ_ROWS = 256
SWA_BLOCKS_PER_STEP = 4
SAMPLE_SEQS_PER_STEP = 8
SAMPLE_SEQS_PER_SCAN_STEP = 4


def _rms(x, g):
    return x * lax.rsqrt(jnp.mean(x * x, axis=-1, keepdims=True) + EPS) * g


def _sigmoid(x):
    return 1.0 / (1.0 + jnp.exp(-x))


def _cparams(*sem):
    return pltpu.CompilerParams(dimension_semantics=sem, vmem_limit_bytes=VMEM_LIMIT_BYTES)


def _lead_tail_specs(n_lead_tiles, cols, row_tile=ROW_TILE):
    return [pl.BlockSpec((row_tile, cols), lambda i: (jnp.minimum(i, n_lead_tiles - 1), 0)),
            pl.BlockSpec((row_tile, cols), lambda i: (jnp.maximum(i - n_lead_tiles, 0), 0))]


def _ffn_body(*refs, n_src, n_dst, n_lead_tiles, cast_next):
    refs = list(refs)
    x_refs = [refs.pop(0) for _ in range(n_src)]
    gpre_ref, gpost_ref, win_ref, wout_ref = [refs.pop(0) for _ in range(4)]
    next_f32 = [refs.pop(0) for _ in range(2)] if cast_next else []
    out_refs = [refs.pop(0) for _ in range(n_dst)]
    next_bf16 = [refs.pop(0) for _ in range(2)] if cast_next else []
    h_ref = refs.pop(0)
    i = pl.program_id(0)
    nf = D_FF // FFN_TILE
    rows = ROW_TILE // FFN_ROW_SPLIT

    for src, dst in zip(next_f32, next_bf16):
        dst[...] = src[...].astype(BF16)

    x_ref = refs.pop(0) if n_src == 2 else x_refs[0]
    out_ref = out_refs[-1]

    for f in range(nf):
        c0, c1 = f * FFN_TILE, (f + 1) * FFN_TILE
        wg = win_ref[:, c0:c1]
        wu = win_ref[:, D_FF + c0:D_FF + c1]
        wo = wout_ref[c0:c1, :]
        for r in range(FFN_ROW_SPLIT):
            rs = slice(r * rows, (r + 1) * rows)
            if f == 0:
                if n_src == 2:
                    x_ref[rs, :] = jnp.where(i < n_lead_tiles, x_refs[0][rs, :], x_refs[1][rs, :])
                h = _rms(x_ref[rs, :], gpre_ref[...]).astype(BF16)
                h_ref[rs, :] = h
            else:
                h = h_ref[rs, :]
            gate = jnp.dot(h, wg, preferred_element_type=F32)
            up = jnp.dot(h, wu, preferred_element_type=F32)
            act = (gate * _sigmoid(gate) * up).astype(BF16)
            part = jnp.dot(act, wo, preferred_element_type=F32)
            if f == 0:
                out_ref[rs, :] = part
            elif f == nf - 1:
                out_ref[rs, :] = x_ref[rs, :] + 0.5 * _rms(out_ref[rs, :] + part, gpost_ref[...])
            else:
                out_ref[rs, :] += part

    if n_dst == 2:
        @pl.when(i < n_lead_tiles)
        def _():
            out_refs[0][...] = out_ref[...]


def _ffn(xs, g_pre, g_post, w_in, w_out, split_rows=None, cast_next=None):
    xs = xs if isinstance(xs, (tuple, list)) else (xs,)
    n = sum(x.shape[0] for x in xs)
    n_tiles = n // ROW_TILE
    assert D_FF // FFN_TILE >= 2
    n_lead = xs[0].shape[0] if len(xs) == 2 else (split_rows[0] if split_rows else n)
    assert split_rows is None or len(xs) == 1 or split_rows[0] == xs[0].shape[0]
    n_lead_tiles = n_lead // ROW_TILE
    resident = pl.Buffered(1)
    row_spec = pl.BlockSpec((ROW_TILE, D_MODEL), lambda i: (i, 0))
    in_specs = _lead_tail_specs(n_lead_tiles, D_MODEL) if len(xs) == 2 else [row_spec]
    in_specs += [
        pl.BlockSpec((1, D_MODEL), lambda i: (0, 0)),
        pl.BlockSpec((1, D_MODEL), lambda i: (0, 0)),
        pl.BlockSpec((D_MODEL, 2 * D_FF), lambda i: (0, 0), pipeline_mode=resident),
        pl.BlockSpec((D_FF, D_MODEL), lambda i: (0, 0), pipeline_mode=resident),
    ]
    args = [*xs, g_pre, g_post, w_in, w_out]
    scratch = [pltpu.VMEM((ROW_TILE, D_MODEL), BF16)]
    if len(xs) == 2:
        scratch.append(pltpu.VMEM((ROW_TILE, D_MODEL), F32))
    if split_rows:
        out_specs = _lead_tail_specs(n_lead_tiles, D_MODEL)
        out_shape = [jax.ShapeDtypeStruct((r, D_MODEL), F32) for r in split_rows]
    else:
        out_specs = [row_spec]
        out_shape = [jax.ShapeDtypeStruct((n, D_MODEL), F32)]
    n_dst = len(out_specs)
    if cast_next:
        wn_in, wn_out, layer, which = cast_next
        assert n_tiles >= WEIGHT_CAST_BLOCKS
        for w in (wn_in, wn_out):
            rows, cols = w.shape[2] // WEIGHT_CAST_BLOCKS, w.shape[3]
            assert rows * WEIGHT_CAST_BLOCKS == w.shape[2]
            in_specs.append(pl.BlockSpec(
                (None, None, rows, cols),
                lambda i: (layer, which, jnp.minimum(i, WEIGHT_CAST_BLOCKS - 1), 0)))
            out_specs.append(pl.BlockSpec((rows, cols), lambda i: (jnp.minimum(i, WEIGHT_CAST_BLOCKS - 1), 0)))
            out_shape.append(jax.ShapeDtypeStruct(w.shape[2:], BF16))
            args.append(w)
    outs = pl.pallas_call(
        functools.partial(_ffn_body, n_src=len(xs), n_dst=n_dst, n_lead_tiles=n_lead_tiles,
                          cast_next=bool(cast_next)),
        grid=(n_tiles,),
        in_specs=in_specs,
        out_specs=out_specs,
        out_shape=out_shape,
        scratch_shapes=scratch,
        compiler_params=_cparams("arbitrary"),
        name="ffn",
    )(*args)
    y = outs[:n_dst] if split_rows else outs[0]
    return y, (tuple(outs[n_dst:]) if cast_next else None)


def _pre_mm_body(x_ref, g_ref, w_ref, b_ref, out_ref, h_ref):
    @pl.when(pl.program_id(1) == 0)
    def _():
        h_ref[...] = _rms(x_ref[...], g_ref[...]).astype(BF16)

    out_ref[...] = jnp.dot(h_ref[...], w_ref[...], preferred_element_type=F32) + b_ref[...]


def _pre_mm(x, g, w, b, name):
    n = x.shape[0]
    n_out = w.shape[1]
    tn = min(n_out, 1024)
    return pl.pallas_call(
        _pre_mm_body,
        grid=(n // ROW_TILE, n_out // tn),
        in_specs=[
            pl.BlockSpec((ROW_TILE, D_MODEL), lambda i, j: (i, 0)),
            pl.BlockSpec((1, D_MODEL), lambda i, j: (0, 0)),
            pl.BlockSpec((D_MODEL, tn), lambda i, j: (0, j)),
            pl.BlockSpec((1, tn), lambda i, j: (0, j)),
        ],
        out_specs=pl.BlockSpec((ROW_TILE, tn), lambda i, j: (i, j)),
        out_shape=jax.ShapeDtypeStruct((n, n_out), F32),
        scratch_shapes=[pltpu.VMEM((ROW_TILE, D_MODEL), BF16)],
        compiler_params=_cparams("parallel", "arbitrary"),
        name=name,
    )(x, g, w, b)


def _hgrn_in_body(x_ref, g_ref, w_ref, z_ref, qig_ref):
    rows = ROW_TILE // 2
    d = D_MODEL
    for r in range(2):
        rs = slice(r * rows, (r + 1) * rows)
        h = _rms(x_ref[rs, :], g_ref[...]).astype(BF16)
        z_ref[rs, :] = jnp.dot(h, w_ref[:, d:2 * d], preferred_element_type=F32)
        for j, c0 in enumerate((0, 2 * d, 3 * d)):
            qig_ref[rs, j * d:(j + 1) * d] = jnp.dot(h, w_ref[:, c0:c0 + d],
                                                     preferred_element_type=F32).astype(BF16)


def _hgrn_in(x, g, w, layer):
    n = x.shape[0]
    return pl.pallas_call(
        _hgrn_in_body,
        grid=(n // ROW_TILE,),
        in_specs=[
            pl.BlockSpec((ROW_TILE, D_MODEL), lambda i: (i, 0)),
            pl.BlockSpec((1, D_MODEL), lambda i: (0, 0)),
            pl.BlockSpec((None, D_MODEL, 4 * D_MODEL), lambda i: (layer, 0, 0), pipeline_mode=pl.Buffered(1)),
        ],
        out_specs=[
            pl.BlockSpec((ROW_TILE, D_MODEL), lambda i: (i, 0)),
            pl.BlockSpec((ROW_TILE, 3 * D_MODEL), lambda i: (i, 0)),
        ],
        out_shape=[
            jax.ShapeDtypeStruct((n, D_MODEL), F32),
            jax.ShapeDtypeStruct((n, 3 * D_MODEL), BF16),
        ],
        compiler_params=_cparams("parallel"),
        name="hgrn_in",
    )(x, g, w)


def _residual_update(o, w_ref, b_ref, g_ref, x):
    return x + _rms(jnp.dot(o, w_ref[...], preferred_element_type=F32) + b_ref[...], g_ref[...])


def _post_mm_body(o_ref, w_ref, b_ref, g_ref, x_ref, out_ref):
    out_ref[...] = _residual_update(o_ref[...], w_ref, b_ref, g_ref, x_ref[...])


def _post_mm_rows(o, w, b, g, x, row0, name):
    tile0 = row0 // ROW_TILE
    assert tile0 * ROW_TILE == row0
    x_spec = pl.BlockSpec((ROW_TILE, D_MODEL), lambda i: (tile0 + i, 0))
    return pl.pallas_call(
        _post_mm_body,
        grid=(o.shape[0] // ROW_TILE,),
        in_specs=[
            pl.BlockSpec((ROW_TILE, D_MODEL), lambda i: (i, 0)),
            pl.BlockSpec((D_MODEL, D_MODEL), lambda i: (0, 0)),
            pl.BlockSpec((1, D_MODEL), lambda i: (0, 0)),
            pl.BlockSpec((1, D_MODEL), lambda i: (0, 0)),
            x_spec,
        ],
        out_specs=x_spec,
        out_shape=jax.ShapeDtypeStruct(x.shape, x.dtype),
        input_output_aliases={4: 0},
        compiler_params=_cparams("parallel"),
        name=name,
    )(o, w, b, g, x)


def _split3(a):
    a1 = a.astype(BF16)
    r1 = a - a1.astype(F32)
    a2 = r1.astype(BF16)
    a3 = (r1 - a2.astype(F32)).astype(BF16)
    return a1, a2, a3


def _hgrn_body(*refs, chunk, subs, seqs, layer, has_init, has_prev_out, fuse_out):
    refs = list(refs)
    q_ref, z_ref, i_ref, g_ref, loglb_ref, log1mlb_ref, onemlb_ref, gn_ref = refs[:8]
    del refs[:8]
    s0_ref = refs.pop(0) if has_init else None
    if has_prev_out:
        refs.pop(0)
    if fuse_out:
        x_ref, wout_ref, bout_ref, gpost_ref = [refs.pop(0) for _ in range(4)]
    o_ref, sout_ref, st_ref = refs
    pending = []
    c = pl.program_id(1)
    n_chunks = pl.num_programs(1)

    @pl.when(c == 0)
    def _():
        if has_init:
            for s in range(seqs):
                for h in range(A_HEADS):
                    st_ref[s * A_HEADS + h] = s0_ref[s, h].T
        else:
            st_ref[...] = jnp.zeros_like(st_ref)

    row = lax.broadcasted_iota(jnp.int32, (chunk, chunk), 0)
    col = lax.broadcasted_iota(jnp.int32, (chunk, chunk), 1)
    causal = col <= row
    tril = causal.astype(BF16)
    mid = chunk // 2
    gn = gn_ref[...]
    log_lb = loglb_ref[...]
    log1m_lb = log1mlb_ref[...]
    onem_lb = onemlb_ref[...]
    dn_t = (((1,), (1,)), ((), ()))
    dn_0 = (((0,), (0,)), ((), ()))

    lanes = [slice(h * A_DK, (h + 1) * A_DK) for h in range(A_HEADS)]

    def decay_stage(r0):
        rs = slice(r0, r0 + chunk)
        q = q_ref[rs, :].astype(F32)
        z = z_ref[rs, :]
        g = g_ref[rs, :].astype(F32)
        vb = i_ref[rs, :]

        e = jnp.exp(-jnp.abs(z))
        r = 1.0 / (1.0 + e)
        log_sig = jnp.minimum(z, 0.0) - jnp.log(1.0 + e)
        b_ = log1m_lb + log_sig
        log_f = jnp.maximum(log_lb, b_) + jnp.log(1.0 + jnp.exp(-jnp.abs(log_lb - b_)))
        k = onem_lb * jnp.where(z >= 0.0, e * r, r)

        f1, f2, f3 = _split3(log_f)
        cum = (jnp.dot(tril, f1, preferred_element_type=F32)
               + jnp.dot(tril, f2, preferred_element_type=F32)
               + jnp.dot(tril, f3, preferred_element_type=F32))
        last = cum[chunk - 1:chunk, :]
        ref_row = cum[mid:mid + 1, :]

        dec = jnp.exp(cum - ref_row)
        q_dec = q * dec
        k_dec = k * (1.0 / dec)
        qs = q_dec.astype(BF16)
        ks = k_dec.astype(BF16)
        qe = (q_dec * jnp.exp(ref_row)).astype(BF16)
        ke = (k_dec * jnp.exp(last - ref_row)).astype(BF16)
        return qs, ks, qe, ke, jnp.exp(last), vb, g * _sigmoid(g)

    def state_stage(s, r0, operands):
        qs, ks, qe, ke, e_last, vb, out_gate = operands
        rs = slice(r0, r0 + chunk)
        scores = [lax.dot_general(qs[:, sl], ks[:, sl], dn_t, preferred_element_type=F32) for sl in lanes]
        sts = [st_ref[s * A_HEADS + h] for h in range(A_HEADS)]
        inter = [lax.dot_general(qe[:, sl], st.astype(BF16), dn_t, preferred_element_type=F32)
                 for sl, st in zip(lanes, sts)]
        kv = [lax.dot_general(vb[:, sl], ke[:, sl], dn_0, preferred_element_type=F32) for sl in lanes]
        probs = [jnp.where(causal, sc, 0.0).astype(BF16) for sc in scores]
        outs = [jnp.dot(p, vb[:, sl], preferred_element_type=F32) + it
                for p, sl, it in zip(probs, lanes, inter)]
        for h, sl in enumerate(lanes):
            st_ref[s * A_HEADS + h] = e_last[:, sl] * sts[h] + kv[h]
        o = jnp.concatenate([_rms(o_h, gn) for o_h in outs], axis=-1)
        gated = (o * out_gate).astype(BF16)
        if not fuse_out:
            o_ref[rs, :] = gated
            return
        pending.append(gated)
        done = r0 + chunk
        if done % OUT_PROJ_ROWS == 0:
            ro = slice(done - OUT_PROJ_ROWS, done)
            o_ref[ro, :] = _residual_update(jnp.concatenate(pending, axis=0), wout_ref, bout_ref, gpost_ref,
                                            x_ref[ro, :])
            pending.clear()

    work = [(s, (s * subs + u) * chunk) for s in range(seqs) for u in range(subs)]
    ready = [decay_stage(r0) for _, r0 in work[:SCAN_LOOKAHEAD]]
    for n, (s, r0) in enumerate(work):
        if n + SCAN_LOOKAHEAD < len(work):
            ready.append(decay_stage(work[n + SCAN_LOOKAHEAD][1]))
        state_stage(s, r0, ready.pop(0))

    @pl.when(c == n_chunks - 1)
    def _():
        own = sout_ref if has_prev_out else sout_ref.at[layer]
        for s in range(seqs):
            for h in range(A_HEADS):
                own[s, h] = st_ref[s * A_HEADS + h].T
        if not has_prev_out:
            for other in range(N_A_LAYERS):
                if other != layer:
                    sout_ref[other] = jnp.zeros(sout_ref.shape[1:], sout_ref.dtype)


def _hgrn_scan(z, qig, lb_terms, gn, s0, s_prev, *, layer, row0, n_seq, seq_len, chunk, subs, seqs, name,
               out_proj=None):
    n_chunks = seq_len // (subs * chunk)
    blk = seqs * subs * chunk
    assert n_chunks == 1 or seqs == 1
    blk0 = row0 // blk
    has_init = s0 is not None
    has_prev = s_prev is not None

    def rows_at(j):
        return pl.BlockSpec((blk, D_MODEL), lambda b, c: (blk0 + b * n_chunks + c, j))

    in_specs = [rows_at(0), rows_at(0), rows_at(1), rows_at(2)]
    in_specs += [pl.BlockSpec((1, D_MODEL), lambda b, c: (0, 0))] * 3
    in_specs += [pl.BlockSpec((1, A_DV), lambda b, c: (0, 0))]
    args = [qig, z, qig, qig, *lb_terms, gn]
    if has_init:
        in_specs.append(pl.BlockSpec((None, seqs, A_HEADS, A_DK, A_DV), lambda b, c: (layer, b, 0, 0, 0)))
        args.append(s0)
    aliases = {}
    if has_prev:
        in_specs.append(pl.BlockSpec(memory_space=pl.ANY))
        aliases = {len(args): 1}
        args.append(s_prev)
    if has_prev:
        state_spec = pl.BlockSpec((None, seqs, A_HEADS, A_DK, A_DV), lambda b, c: (layer, b, 0, 0, 0))
    else:
        state_spec = pl.BlockSpec((N_A_LAYERS, seqs, A_HEADS, A_DK, A_DV), lambda b, c: (0, b, 0, 0, 0))
    o_spec = pl.BlockSpec((blk, D_MODEL), lambda b, c: (b * n_chunks + c, 0))
    scratch = [pltpu.VMEM((seqs * A_HEADS, A_DV, A_DK), F32)]
    if out_proj is not None:
        x, w_out, b_out, g_post = out_proj
        assert row0 == 0 and blk % OUT_PROJ_ROWS == 0
        aliases[len(args)] = 0
        in_specs += [o_spec,
                     pl.BlockSpec((D_MODEL, D_MODEL), lambda b, c: (0, 0), pipeline_mode=pl.Buffered(1)),
                     pl.BlockSpec((1, D_MODEL), lambda b, c: (0, 0)),
                     pl.BlockSpec((1, D_MODEL), lambda b, c: (0, 0))]
        args += [x, w_out, b_out, g_post]
        o_shape = jax.ShapeDtypeStruct(x.shape, x.dtype)
    else:
        o_shape = jax.ShapeDtypeStruct((n_seq * seq_len, D_MODEL), BF16)
    return pl.pallas_call(
        functools.partial(_hgrn_body, chunk=chunk, subs=subs, seqs=seqs, layer=layer, has_init=has_init,
                          has_prev_out=has_prev, fuse_out=out_proj is not None),
        grid=(n_seq // seqs, n_chunks),
        in_specs=in_specs,
        out_specs=[o_spec, state_spec],
        out_shape=[o_shape, jax.ShapeDtypeStruct((N_A_LAYERS, n_seq, A_HEADS, A_DK, A_DV), F32)],
        scratch_shapes=scratch,
        input_output_aliases=aliases,
        compiler_params=_cparams("parallel", "arbitrary"),
        name=name,
    )(*args)


def _swa_prompt_body(sink_ref, q_ref, kvp_ref, kvc_ref, x_ref, wo_ref, bo_ref, gpost_ref, xnew_ref):
    i = pl.program_id(1)
    nq = q_ref.shape[0] // WINDOW
    key = lax.broadcasted_iota(jnp.int32, (2 * WINDOW, WINDOW), 0)
    qry = lax.broadcasted_iota(jnp.int32, (2 * WINDOW, WINDOW), 1)
    band = ((key < WINDOW) & (key >= qry)) | ((key >= WINDOW) & (key - WINDOW <= qry))
    first_band = band & ((key >= WINDOW) | (i > 0))
    scale = B_HEAD_DIM ** -0.5
    kv_w = B_KV_HEADS * B_HEAD_DIM
    dn_t = (((1,), (1,)), ((), ()))
    dn_0 = (((0,), (0,)), ((), ()))
    pairs = [(j, kh) for j in range(nq) for kh in range(B_KV_HEADS)]

    kv_rows = jnp.concatenate([kvp_ref[...], kvc_ref[...]], axis=0).astype(BF16)
    scores = []
    for j, kh in pairs:
        k = kv_rows[j * WINDOW:(j + 2) * WINDOW, kh * B_HEAD_DIM:(kh + 1) * B_HEAD_DIM]
        heads = range(kh * B_GROUP, (kh + 1) * B_GROUP)
        q = jnp.concatenate([q_ref[j * WINDOW:(j + 1) * WINDOW, h * B_HEAD_DIM:(h + 1) * B_HEAD_DIM]
                             for h in heads], axis=0)
        q = (q * scale).astype(BF16)
        scores.append(lax.dot_general(k, q, dn_t, preferred_element_type=F32))
    probs = []
    for (j, kh), s_t in zip(pairs, scores):
        mask = first_band if j == 0 else band
        p_parts, inv_parts = [], []
        for gi in range(B_GROUP):
            s = jnp.where(mask, s_t[:, gi * WINDOW:(gi + 1) * WINDOW], NEG_INF)
            sink = sink_ref[kh * B_GROUP + gi]
            m = jnp.maximum(jnp.max(s, axis=0, keepdims=True), sink)
            p = jnp.exp(s - m)
            inv_parts.append(1.0 / (jnp.sum(p, axis=0, keepdims=True) + jnp.exp(sink - m)))
            p_parts.append(p.astype(BF16))
        probs.append((jnp.concatenate(p_parts, axis=1), jnp.concatenate(inv_parts, axis=1)))
    outs = {}
    for (j, kh), (p_t, inv) in zip(pairs, probs):
        v = kv_rows[j * WINDOW:(j + 2) * WINDOW, kv_w + kh * B_HEAD_DIM:kv_w + (kh + 1) * B_HEAD_DIM]
        outs[j, kh] = lax.dot_general(v, p_t, dn_0, preferred_element_type=F32) * inv
    pending = []
    for j in range(nq):
        o_t = jnp.concatenate([outs[j, kh][:, gi * WINDOW:(gi + 1) * WINDOW]
                               for kh in range(B_KV_HEADS) for gi in range(B_GROUP)], axis=0)
        pending.append(o_t.T.astype(BF16))
        done = (j + 1) * WINDOW
        if done % OUT_PROJ_ROWS == 0:
            ro = slice(done - OUT_PROJ_ROWS, done)
            xnew_ref[ro, :] = _residual_update(jnp.concatenate(pending, axis=0), wo_ref, bo_ref, gpost_ref,
                                               x_ref[ro, :])
            pending.clear()


def _swa_prompt(q, kv, sinks, x, w_o, b_o, g_post, *, n_seq, seq_len):
    nq = math.gcd(seq_len // WINDOW, SWA_BLOCKS_PER_STEP)
    rows = nq * WINDOW
    assert rows % OUT_PROJ_ROWS == 0
    nb = seq_len // rows
    kv_cols = kv.shape[1]
    row_spec = pl.BlockSpec((rows, D_MODEL), lambda b, i: (b * nb + i, 0))
    return pl.pallas_call(
        _swa_prompt_body,
        grid=(n_seq, nb),
        in_specs=[
            pl.BlockSpec(memory_space=pltpu.SMEM),
            row_spec,
            pl.BlockSpec((WINDOW, kv_cols), lambda b, i: (jnp.maximum((b * nb + i) * nq - 1, 0), 0)),
            pl.BlockSpec((rows, kv_cols), lambda b, i: (b * nb + i, 0)),
            row_spec,
            pl.BlockSpec((D_MODEL, D_MODEL), lambda b, i: (0, 0), pipeline_mode=pl.Buffered(1)),
            pl.BlockSpec((1, D_MODEL), lambda b, i: (0, 0)),
            pl.BlockSpec((1, D_MODEL), lambda b, i: (0, 0)),
        ],
        out_specs=row_spec,
        out_shape=jax.ShapeDtypeStruct(x.shape, x.dtype),
        input_output_aliases={4: 0},
        compiler_params=_cparams("parallel", "arbitrary"),
        name="swa_prompt",
    )(sinks, q, kv, kv, x, w_o, b_o, g_post)


def _swa_sample_body(sink_ref, q_ref, kc_ref, vc_ref, kvn_ref, o_ref, *, t_new):
    rows = B_GROUP * t_new
    t_row = lax.broadcasted_iota(jnp.int32, (rows, WINDOW), 0) % t_new
    c_col = lax.broadcasted_iota(jnp.int32, (rows, WINDOW), 1)
    mask_cache = c_col >= t_row
    t_row_n = lax.broadcasted_iota(jnp.int32, (rows, t_new), 0) % t_new
    t_col_n = lax.broadcasted_iota(jnp.int32, (rows, t_new), 1)
    mask_new = t_col_n <= t_row_n
    scale = B_HEAD_DIM ** -0.5
    kv_w = B_KV_HEADS * B_HEAD_DIM
    dn = (((1,), (1,)), ((), ()))
    pairs = [(s, kh) for s in range(SAMPLE_SEQS_PER_STEP) for kh in range(B_KV_HEADS)]
    scores = []
    for s, kh in pairs:
        ksl = slice(kh * B_HEAD_DIM, (kh + 1) * B_HEAD_DIM)
        q = (q_ref[s, kh] * scale).astype(BF16)
        s_c = lax.dot_general(q, kc_ref[s, :, ksl].astype(BF16), dn, preferred_element_type=F32)
        s_n = lax.dot_general(q, kvn_ref[s, :, ksl].astype(BF16), dn, preferred_element_type=F32)
        scores.append((s_c, s_n))
    probs = []
    for (s, kh), (s_c, s_n) in zip(pairs, scores):
        s_c = jnp.where(mask_cache, s_c, NEG_INF)
        s_n = jnp.where(mask_new, s_n, NEG_INF)
        sink = sink_ref[kh]
        m = jnp.maximum(jnp.maximum(jnp.max(s_c, axis=-1, keepdims=True),
                                    jnp.max(s_n, axis=-1, keepdims=True)), sink)
        p_c = jnp.exp(s_c - m)
        p_n = jnp.exp(s_n - m)
        denom = (jnp.sum(p_c, axis=-1, keepdims=True) + jnp.sum(p_n, axis=-1, keepdims=True)
                 + jnp.exp(sink - m))
        probs.append((p_c.astype(BF16), p_n.astype(BF16), 1.0 / denom))
    for (s, kh), (p_c, p_n, inv) in zip(pairs, probs):
        ksl = slice(kh * B_HEAD_DIM, (kh + 1) * B_HEAD_DIM)
        vsl = slice(kv_w + kh * B_HEAD_DIM, kv_w + (kh + 1) * B_HEAD_DIM)
        pv = (jnp.dot(p_c, vc_ref[s, :, ksl].astype(BF16), preferred_element_type=F32)
              + jnp.dot(p_n, kvn_ref[s, :, vsl].astype(BF16), preferred_element_type=F32))
        o_ref[s, kh] = (pv * inv).astype(o_ref.dtype)


def _swa_sample(q, cache_k, cache_v, kv_new, sink_rows, *, t_new):
    n_seq = q.shape[0]
    rows = B_GROUP * t_new
    sb = SAMPLE_SEQS_PER_STEP
    kv_w = B_KV_HEADS * B_HEAD_DIM
    return pl.pallas_call(
        functools.partial(_swa_sample_body, t_new=t_new),
        grid=(n_seq // sb,),
        in_specs=[
            pl.BlockSpec((B_KV_HEADS, rows, 1), lambda b: (0, 0, 0)),
            pl.BlockSpec((sb, B_KV_HEADS, rows, B_HEAD_DIM), lambda b: (b, 0, 0, 0)),
            pl.BlockSpec((sb, WINDOW, kv_w), lambda b: (b, 0, 0)),
            pl.BlockSpec((sb, WINDOW, kv_w), lambda b: (b, 0, 0)),
            pl.BlockSpec((sb, t_new, 2 * kv_w), lambda b: (b, 0, 0)),
        ],
        out_specs=pl.BlockSpec((sb, B_KV_HEADS, rows, B_HEAD_DIM), lambda b: (b, 0, 0, 0)),
        out_shape=jax.ShapeDtypeStruct((n_seq, B_KV_HEADS, rows, B_HEAD_DIM), BF16),
        compiler_params=_cparams("parallel"),
        name="swa_sample",
    )(sink_rows, q, cache_k, cache_v, kv_new)


def kernel(x_prompt, x_sample, state_hgrn, cache_k_window, cache_v_window, norm_g, ffn_w_in, ffn_w_out,
           a_w_in, a_lower_bounds, a_gnorm, a_w_out, kv_norm, kv_w, kv_b, b_w_q, b_b_q, b_sinks, b_w_o, b_b_o):
    n_pb, p_len, _ = x_prompt.shape
    n_sb, s_len, _ = x_sample.shape
    n_p = n_pb * p_len
    n_s = n_sb * s_len
    kv_cols = B_KV_HEADS * B_HEAD_DIM

    x = (x_prompt.reshape(n_p, D_MODEL), x_sample.reshape(n_s, D_MODEL))

    ffn_w = (ffn_w_in[0, 0].astype(BF16), ffn_w_out[0, 0].astype(BF16))
    a_w_in_b = a_w_in.astype(BF16)
    a_w_out_b = a_w_out.astype(BF16)
    kv_w_b = kv_w.astype(BF16)
    b_w_q_b = b_w_q.astype(BF16)
    b_w_o_b = b_w_o.astype(BF16)
    zero_bias = jnp.zeros((1, D_MODEL), F32)

    lb_soft = jax.nn.softmax(a_lower_bounds.astype(F32), axis=0)
    lbs = jnp.cumsum(lb_soft, axis=0) - lb_soft[0]
    pos = lbs > 0
    log_lb = jnp.where(pos, jnp.log(jnp.where(pos, lbs, 1.0)), NEG_INF)
    log1m_lb = jnp.log1p(-lbs)
    onem_lb = 1.0 - lbs

    cache_k = cache_k_window.reshape(n_sb, WINDOW, kv_cols)
    cache_v = cache_v_window.reshape(n_sb, WINDOW, kv_cols)

    def g_row(l, j):
        return norm_g[l, j].reshape(1, D_MODEL)

    s_prompt = s_sample = None
    kv = None
    for l in range(DEPTH):
        x, ffn_w = _ffn(x, g_row(l, 0), g_row(l, 1), *ffn_w, cast_next=(ffn_w_in, ffn_w_out, l, 1))
        if l < N_A_LAYERS:
            z, qig = _hgrn_in(x, g_row(l, 2), a_w_in_b, l)
            lb_terms = (log_lb[l].reshape(1, D_MODEL), log1m_lb[l].reshape(1, D_MODEL),
                        onem_lb[l].reshape(1, D_MODEL))
            gn = a_gnorm[l].reshape(1, A_DV)
            p_chunk = math.gcd(p_len, PROMPT_CHUNK)
            out_proj = (a_w_out_b[l], zero_bias, g_row(l, 3))
            x, s_prompt = _hgrn_scan(z, qig, lb_terms, gn, None, s_prompt, layer=l, row0=0, n_seq=n_pb,
                                     seq_len=p_len, chunk=p_chunk,
                                     subs=math.gcd(p_len // p_chunk, PROMPT_CHUNKS_PER_STEP), seqs=1,
                                     name="hgrn_prompt", out_proj=(x, *out_proj))
            o_s, s_sample = _hgrn_scan(z, qig, lb_terms, gn, state_hgrn, s_sample, layer=l, row0=n_p,
                                       n_seq=n_sb, seq_len=s_len, chunk=s_len, subs=1,
                                       seqs=SAMPLE_SEQS_PER_SCAN_STEP, name="hgrn_sample")
            x = _post_mm_rows(o_s, *out_proj, x, n_p, "hgrn_out")
        else:
            j = l - N_A_LAYERS
            out_proj = (b_w_o_b[j], b_b_o[j].reshape(1, D_MODEL), g_row(l, 3))
            q = _pre_mm(x, g_row(l, 2), b_w_q_b[j], b_b_q[j].reshape(1, D_MODEL), "swa_q")
            x = _swa_prompt(q, kv, b_sinks[j], x, *out_proj, n_seq=n_pb, seq_len=p_len)
            q_s = q[n_p:].reshape(n_sb, s_len, B_KV_HEADS, B_GROUP, B_HEAD_DIM)
            q_s = q_s.transpose(0, 2, 3, 1, 4).reshape(n_sb, B_KV_HEADS, B_GROUP * s_len, B_HEAD_DIM)
            sink_rows = jnp.repeat(b_sinks[j].reshape(B_KV_HEADS, B_GROUP), s_len, axis=1)[..., None]
            o_s = _swa_sample(q_s, cache_k, cache_v, kv[n_p:].reshape(n_sb, s_len, 2 * kv_cols),
                              sink_rows, t_new=s_len)
            o_s = o_s.reshape(n_sb, B_KV_HEADS, B_GROUP, s_len, B_HEAD_DIM)
            o_s = o_s.transpose(0, 3, 1, 2, 4).reshape(n_s, D_MODEL)
            x = _post_mm_rows(o_s, *out_proj, x, n_p, "swa_out")
        last = l == DEPTH - 1
        x, ffn_w = _ffn(x, g_row(l, 4), g_row(l, 5), *ffn_w, split_rows=(n_p, n_s) if last else None,
                        cast_next=None if last else (ffn_w_in, ffn_w_out, l + 1, 0))
        if l == N_A_LAYERS - 1:
            kv = _pre_mm(x, kv_norm.reshape(1, D_MODEL), kv_w_b, kv_b.reshape(1, 2 * kv_cols), "shared_kv")

    y_prompt = x[0].reshape(n_pb, p_len, D_MODEL)
    y_sample = x[1].reshape(n_sb, s_len, D_MODEL)

    kv_p = jnp.stack([kv[(b + 1) * p_len - WINDOW:(b + 1) * p_len] for b in range(n_pb)])
    kv_p = kv_p.reshape(n_pb, WINDOW, 2, B_KV_HEADS, B_HEAD_DIM)
    kv_s = kv[n_p:].reshape(n_sb, s_len, 2, B_KV_HEADS, B_HEAD_DIM)
    k_win_sample = jnp.concatenate([cache_k_window, kv_s[:, :, 0]], axis=1)[:, -WINDOW:]
    v_win_sample = jnp.concatenate([cache_v_window, kv_s[:, :, 1]], axis=1)[:, -WINDOW:]
    return (y_prompt, y_sample, s_prompt, s_sample,
            kv_p[:, :, 0], kv_p[:, :, 1], k_win_sample, v_win_sample)
```
